```python
import jax, jax.numpy as jnp
from jax import lax
import numpy as np

D_MODEL = 1024
BATCH = 1
SEQ = 16384
DEPTH = 2
DEC_BATCH = 32
DEC_SEQ = 8
PAST_LEN = 16384
PAGE_SIZE = 128

D_CONV = D_MODEL
CONV_WIDTH = 31
HEAD_DIM = 64
HEADS_PER_GROUP = 4
ATTN_GROUPS = ((128, 1), (512, 4), (2048, 16))
N_GROUPS = len(ATTN_GROUPS)
D_ATTN = N_GROUPS * HEADS_PER_GROUP * HEAD_DIM
D_ATTN_OUT = HEADS_PER_GROUP * HEAD_DIM
N_BRANCHES = 2
D_IN = 2 * D_CONV + 3 * D_ATTN + N_BRANCHES * D_MODEL
D_FF = 3 * D_MODEL
FFN_CONV_WIDTH = 3
Q_BLOCK = 128
EPS = 1e-6

kernel_name = "hybrid_conformer_dilated_attn_decoder_step"


def rms_norm(x, g):
    xf = x.astype(jnp.float32)
    y = xf * lax.rsqrt(jnp.mean(xf * xf, axis=-1, keepdims=True) + EPS)
    return (y * g.astype(jnp.float32)).astype(x.dtype)


def layer_norm(x, g, b):
    xf = x.astype(jnp.float32)
    mu = jnp.mean(xf, axis=-1, keepdims=True)
    xc = xf - mu
    var = jnp.mean(xc * xc, axis=-1, keepdims=True)
    y = xc * lax.rsqrt(var + EPS) * g.astype(jnp.float32) + b.astype(jnp.float32)
    return y.astype(x.dtype)


def causal_depthwise_conv(u, hist, w, b):
    width = w.shape[0]
    xp = jnp.concatenate([hist.astype(u.dtype), u], axis=1)
    y = lax.conv_general_dilated(xp, w[:, None, :].astype(u.dtype), window_strides=(1,), padding='VALID',
                                 dimension_numbers=('NWC', 'WIO', 'NWC'), feature_group_count=u.shape[-1])
    return y + b.astype(u.dtype), xp[:, xp.shape[1] - (width - 1):]


def dilated_window_attention(q, k_ext, v_ext, offset, dilation, n_keys):
    n, t, h, hd = q.shape
    qb = Q_BLOCK if t % Q_BLOCK == 0 else t
    n_blocks = t // qb
    scale = HEAD_DIM ** -0.5
    steps = jnp.arange(n_keys, dtype=jnp.int32) * dilation

    def block(bi):
        q0 = bi * qb
        pos = offset + q0 + jnp.arange(qb, dtype=jnp.int32)
        kidx = pos[:, None] - steps[None, :]
        valid = kidx >= 0
        kidx = jnp.maximum(kidx, 0)
        kg = jnp.take(k_ext, kidx, axis=1)
        vg = jnp.take(v_ext, kidx, axis=1)
        qblk = lax.dynamic_slice_in_dim(q, q0, qb, axis=1)
        s = jnp.einsum('nqhd,nqkhd->nhqk', qblk, kg, preferred_element_type=jnp.float32) * scale
        s = jnp.where(valid[None, None], s, -jnp.inf)
        m = jnp.max(s, axis=-1, keepdims=True)
        p = jnp.exp(s - m)
        l = jnp.sum(p, axis=-1, keepdims=True)
        o = jnp.einsum('nhqk,nqkhd->nqhd', p, vg.astype(jnp.float32))
        o = o / jnp.transpose(l, (0, 2, 1, 3))
        lse = jnp.transpose((m + jnp.log(l))[..., 0], (0, 2, 1))
        return o, lse

    o, lse = lax.map(block, jnp.arange(n_blocks, dtype=jnp.int32))
    o = jnp.transpose(o, (1, 0, 2, 3, 4)).reshape(n, t, h, hd)
    lse = jnp.transpose(lse, (1, 0, 2, 3)).reshape(n, t, h)
    return o, lse


def trunk_layer(x, conv_hist, kv_hist, ffn_hist, norm_attn_g, w_in, conv_dw_w, conv_dw_b, conv_ln_g,
                conv_ln_b, w_conv_out, w_attn_out, w_out, norm_ffn_g, w_up, ffn_dw_w, ffn_dw_b, w_down):
    n, t, _ = x.shape
    h = rms_norm(x, norm_attn_g)
    proj = jnp.einsum('ntd,de->nte', h, w_in)
    glu_in, q, k, v, gates = jnp.split(
        proj, [2 * D_CONV, 2 * D_CONV + D_ATTN, 2 * D_CONV + 2 * D_ATTN, 2 * D_CONV + 3 * D_ATTN], axis=-1)

    a, b = jnp.split(glu_in, 2, axis=-1)
    u = a * jax.nn.sigmoid(b)
    c, new_conv = causal_depthwise_conv(u, conv_hist, conv_dw_w, conv_dw_b)
    c = jax.nn.silu(layer_norm(c, conv_ln_g, conv_ln_b))
    branch_conv = jnp.einsum('ntc,cd->ntd', c, w_conv_out)

    q = q.reshape(n, t, N_GROUPS, HEADS_PER_GROUP, HEAD_DIM)
    k = k.reshape(n, t, N_GROUPS, HEADS_PER_GROUP, HEAD_DIM)
    v = v.reshape(n, t, N_GROUPS, HEADS_PER_GROUP, HEAD_DIM)
    outs, lses, new_kv = [], [], []
    for g, (window, dilation) in enumerate(ATTN_GROUPS):
        k_hist, v_hist = kv_hist[g]
        hist_len = k_hist.shape[1]
        k_ext = jnp.concatenate([k_hist.astype(k.dtype), k[:, :, g]], axis=1)
        v_ext = jnp.concatenate([v_hist.astype(v.dtype), v[:, :, g]], axis=1)
        o, lse = dilated_window_attention(q[:, :, g], k_ext, v_ext, hist_len, dilation, window // dilation + 1)
        outs.append(o)
        lses.append(lse)
        keep = hist_len if hist_len > 0 else min(window, t)
        new_kv.append((k_ext[:, k_ext.shape[1] - keep:], v_ext[:, v_ext.shape[1] - keep:]))
    wts = jax.nn.softmax(jnp.stack(lses, axis=0), axis=0)
    attn = jnp.sum(wts[..., None] * jnp.stack(outs, axis=0), axis=0)
    attn = attn.reshape(n, t, D_ATTN_OUT).astype(x.dtype)
    branch_attn = jnp.einsum('nta,ad->ntd', attn, w_attn_out)

    g_conv, g_attn = jnp.split(gates, 2, axis=-1)
    mixed = jax.nn.sigmoid(g_conv) * branch_conv + jax.nn.sigmoid(g_attn) * branch_attn
    x = x + jnp.einsum('ntd,de->nte', mixed, w_out)

    h = rms_norm(x, norm_ffn_g)
    up = jnp.einsum('ntd,df->ntf', h, w_up)
    up_c, new_ffn = causal_depthwise_conv(up, ffn_hist, ffn_dw_w, ffn_dw_b)
    val, gate = jnp.split(up_c, 2, axis=-1)
    x = x + jnp.einsum('ntf,fd->ntd', jax.nn.silu(gate) * val, w_down)
    return x, new_conv, new_kv, new_ffn


def setup_inputs(seed: int = 0) -> dict:
    key = jax.random.key(seed)
    ks = jax.random.split(key, 32)
    f32 = jnp.float32

    def nrm(k, shape, scale):
        return jax.random.normal(k, shape, f32) * scale

    bufs = [min(w, PAST_LEN) for w, _ in ATTN_GROUPS]
    kv_shape = lambda L: (DEPTH, DEC_BATCH, L, HEADS_PER_GROUP, HEAD_DIM)
    return {
        "x_prompt": nrm(ks[0], (BATCH, SEQ, D_MODEL), 1.0),
        "x_sample": nrm(ks[1], (DEC_BATCH, DEC_SEQ, D_MODEL), 1.0),
        "state_conv": nrm(ks[2], (DEPTH, DEC_BATCH, CONV_WIDTH - 1, D_CONV), 0.5),
        "cache_k_w128": nrm(ks[3], kv_shape(bufs[0]), 1.0),
        "cache_v_w128": nrm(ks[4], kv_shape(bufs[0]), 1.0),
        "cache_k_w512": nrm(ks[5], kv_shape(bufs[1]), 1.0),
        "cache_v_w512": nrm(ks[6], kv_shape(bufs[1]), 1.0),
        "cache_k_w2048": nrm(ks[7], kv_shape(bufs[2]), 1.0),
        "cache_v_w2048": nrm(ks[8], kv_shape(bufs[2]), 1.0),
        "state_ffn_conv": nrm(ks[9], (DEPTH, DEC_BATCH, FFN_CONV_WIDTH - 1, 2 * D_FF), 1.0),
        "norm_attn_g": 1.0 + nrm(ks[10], (DEPTH, D_MODEL), 0.01),
        "w_in": nrm(ks[11], (DEPTH, D_MODEL, D_IN), D_MODEL ** -0.5),
        "conv_dw_w": nrm(ks[12], (DEPTH, CONV_WIDTH, D_CONV), CONV_WIDTH ** -0.5),
        "conv_dw_b": nrm(ks[13], (DEPTH, D_CONV), 0.01),
        "conv_ln_g": 1.0 + nrm(ks[14], (DEPTH, D_CONV), 0.01),
        "conv_ln_b": nrm(ks[15], (DEPTH, D_CONV), 0.01),
        "w_conv_out": nrm(ks[16], (DEPTH, D_CONV, D_MODEL), D_CONV ** -0.5),
        "w_attn_out": nrm(ks[17], (DEPTH, D_ATTN_OUT, D_MODEL), D_ATTN_OUT ** -0.5),
        "w_out": nrm(ks[18], (DEPTH, D_MODEL, D_MODEL), D_MODEL ** -0.5),
        "norm_ffn_g": 1.0 + nrm(ks[19], (DEPTH, D_MODEL), 0.01),
        "w_up": nrm(ks[20], (DEPTH, D_MODEL, 2 * D_FF), D_MODEL ** -0.5),
        "ffn_dw_w": nrm(ks[21], (DEPTH, FFN_CONV_WIDTH, 2 * D_FF), FFN_CONV_WIDTH ** -0.5),
        "ffn_dw_b": nrm(ks[22], (DEPTH, 2 * D_FF), 0.01),
        "w_down": nrm(ks[23], (DEPTH, D_FF, D_MODEL), D_FF ** -0.5),
        "norm_final_g": 1.0 + nrm(ks[24], (D_MODEL,), 0.01),
    }


def reference(x_prompt, x_sample, state_conv, cache_k_w128, cache_v_w128, cache_k_w512, cache_v_w512,
              cache_k_w2048, cache_v_w2048, state_ffn_conv, norm_attn_g, w_in, conv_dw_w, conv_dw_b,
              conv_ln_g, conv_ln_b, w_conv_out, w_attn_out, w_out, norm_ffn_g, w_up, ffn_dw_w, ffn_dw_b,
              w_down, norm_final_g):
    nb = x_prompt.shape[0]
    dt = x_prompt.dtype
    sample_kv = ((cache_k_w128, cache_v_w128), (cache_k_w512, cache_v_w512), (cache_k_w2048, cache_v_w2048))
    xp, xs = x_prompt, x_sample
    conv_p, conv_s, ffn_p, ffn_s = [], [], [], []
    kv_p = [[[], []] for _ in range(N_GROUPS)]
    kv_s = [[[], []] for _ in range(N_GROUPS)]
    for l in range(DEPTH):
        params = (norm_attn_g[l], w_in[l], conv_dw_w[l], conv_dw_b[l], conv_ln_g[l], conv_ln_b[l],
                  w_conv_out[l], w_attn_out[l], w_out[l], norm_ffn_g[l], w_up[l], ffn_dw_w[l], ffn_dw_b[l],
                  w_down[l])
        hist_kv_p = [(jnp.zeros((nb, 0, HEADS_PER_GROUP, HEAD_DIM), dt),
                      jnp.zeros((nb, 0, HEADS_PER_GROUP, HEAD_DIM), dt)) for _ in range(N_GROUPS)]
        xp, c_p, nkv_p, f_p = trunk_layer(xp, jnp.zeros((nb, CONV_WIDTH - 1, D_CONV), dt), hist_kv_p,
                                          jnp.zeros((nb, FFN_CONV_WIDTH - 1, 2 * D_FF), dt), *params)
        hist_kv_s = [(kc[l], vc[l]) for kc, vc in sample_kv]
        xs, c_s, nkv_s, f_s = trunk_layer(xs, state_conv[l], hist_kv_s, state_ffn_conv[l], *params)
        conv_p.append(c_p)
        conv_s.append(c_s)
        ffn_p.append(f_p)
        ffn_s.append(f_s)
        for g in range(N_GROUPS):
            kv_p[g][0].append(nkv_p[g][0])
            kv_p[g][1].append(nkv_p[g][1])
            kv_s[g][0].append(nkv_s[g][0])
            kv_s[g][1].append(nkv_s[g][1])
    y_prompt = rms_norm(xp, norm_final_g)
    y_sample = rms_norm(xs, norm_final_g)
    st = lambda lst: jnp.stack(lst, axis=0)
    return (y_prompt, y_sample,
            st(conv_p), st(conv_s),
            st(kv_p[0][0]), st(kv_p[0][1]), st(kv_s[0][0]), st(kv_s[0][1]),
            st(kv_p[1][0]), st(kv_p[1][1]), st(kv_s[1][0]), st(kv_s[1][1]),
            st(kv_p[2][0]), st(kv_p[2][1]), st(kv_s[2][0]), st(kv_s[2][1]),
            st(ffn_p), st(ffn_s))
```

```python
import functools

import jax
import jax.numpy as jnp
from jax import lax
from jax.experimental import pallas as pl
from jax.experimental.pallas import tpu as pltpu

F32 = jnp.float32
BF16 = jnp.bfloat16

D_MODEL = 1024
D_CONV = 1024
CONV_WIDTH = 31
HEAD_DIM = 64
HEADS_PER_GROUP = 4
ATTN_GROUPS = ((128, 1), (512, 4), (2048, 16))
N_KEYS = 129
D_GROUP = HEADS_PER_GROUP * HEAD_DIM
D_ATTN = len(ATTN_GROUPS) * D_GROUP
D_FF = 3 * D_MODEL
FFN_CONV_WIDTH = 3
EPS = 1e-6
SCALE = HEAD_DIM ** -0.5

LANES = 128
SUBLANES = 8
VMEM_LIMIT = 56 * 1024 * 1024

CONV_HALO = 32
ATTN_TILE = 2048
ATTN_BLOCK = 128


def _const_spec(shape):
    nd = len(shape)
    return pl.BlockSpec(shape, lambda *_: (0,) * nd, pipeline_mode=pl.Buffered(1))


def _rms(x, g):
    return x * lax.rsqrt(jnp.mean(x * x, axis=-1, keepdims=True) + EPS) * g


def _inproj_body(x_ref, g_ref, w_ref, u_ref, qkv_ref, sg_ref):
    h = _rms(x_ref[...], g_ref[...]).astype(BF16)

    def mm(c0, c1):
        return jnp.dot(h, w_ref[:, c0:c1], preferred_element_type=F32)

    for j in range(0, D_CONV, 512):
        a = mm(j, j + 512)
        b = mm(D_CONV + j, D_CONV + j + 512)
        u_ref[:, j:j + 512] = a * jax.nn.sigmoid(b)
    q0 = 2 * D_CONV
    for j in range(0, 3 * D_ATTN, D_ATTN):
        qkv_ref[:, j:j + D_ATTN] = mm(q0 + j, q0 + j + D_ATTN)
    g0 = q0 + 3 * D_ATTN
    for j in range(0, 2 * D_MODEL, 512):
        sg_ref[:, j:j + 512] = jax.nn.sigmoid(mm(g0 + j, g0 + j + 512))


def _inproj(x, g, w_in, tm):
    rows = x.shape[0]
    d_in = w_in.shape[1]
    return pl.pallas_call(
        _inproj_body,
        grid=(rows // tm,),
        in_specs=[
            pl.BlockSpec((tm, D_MODEL), lambda i: (i, 0)),
            _const_spec((1, D_MODEL)),
            _const_spec((D_MODEL, d_in)),
        ],
        out_specs=[
            pl.BlockSpec((tm, D_CONV), lambda i: (i, 0)),
            pl.BlockSpec((tm, 3 * D_ATTN), lambda i: (i, 0)),
            pl.BlockSpec((tm, 2 * D_MODEL), lambda i: (i, 0)),
        ],
        out_shape=[
            jax.ShapeDtypeStruct((rows, D_CONV), F32),
            jax.ShapeDtypeStruct((rows, 3 * D_ATTN), F32),
            jax.ShapeDtypeStruct((rows, 2 * D_MODEL), F32),
        ],
        compiler_params=pltpu.CompilerParams(
            dimension_semantics=("parallel",), vmem_limit_bytes=VMEM_LIMIT),
        name="inproj",
    )(x, g, w_in)


def _ln_swish(y, g, b):
    mu = jnp.mean(y, axis=-1, keepdims=True)
    yc = y - mu
    var = jnp.mean(yc * yc, axis=-1, keepdims=True)
    z = yc * lax.rsqrt(var + EPS) * g + b
    return z * jax.nn.sigmoid(z)


def _conv_prompt_body(tm, u_ref, w_ref, b_ref, lng_ref, lnb_ref, c_ref, buf, ybuf):
    @pl.when(pl.program_id(0) == 0)
    def _():
        buf[0:CONV_HALO, :] = jnp.zeros((CONV_HALO, D_CONV), F32)

    buf[CONV_HALO:CONV_HALO + tm, :] = u_ref[...]
    first = CONV_HALO - (CONV_WIDTH - 1)
    rc = 64
    for l0 in range(0, D_CONV, LANES):
        for r0 in range(0, tm, rc):
            acc = jnp.broadcast_to(b_ref[:, l0:l0 + LANES], (rc, LANES))
            for k in range(CONV_WIDTH):
                acc = acc + w_ref[k:k + 1, l0:l0 + LANES] * buf[r0 + first + k:r0 + first + k + rc, l0:l0 + LANES]
            ybuf[r0:r0 + rc, l0:l0 + LANES] = acc
    c_ref[...] = _ln_swish(ybuf[...], lng_ref[...], lnb_ref[...]).astype(c_ref.dtype)
    buf[0:CONV_HALO, :] = buf[tm:tm + CONV_HALO, :]


def _conv_prompt(u, w, b, lng, lnb, tm):
    rows = u.shape[0]
    return pl.pallas_call(
        functools.partial(_conv_prompt_body, tm),
        grid=(rows // tm,),
        in_specs=[
            pl.BlockSpec((tm, D_CONV), lambda i: (i, 0)),
            _const_spec((CONV_WIDTH, D_CONV)),
            _const_spec((1, D_CONV)),
            _const_spec((1, D_CONV)),
            _const_spec((1, D_CONV)),
        ],
        out_specs=pl.BlockSpec((tm, D_CONV), lambda i: (i, 0)),
        out_shape=jax.ShapeDtypeStruct((rows, D_CONV), BF16),
        scratch_shapes=[
            pltpu.VMEM((CONV_HALO + tm, D_CONV), F32),
            pltpu.VMEM((tm, D_CONV), F32),
        ],
        compiler_params=pltpu.CompilerParams(
            dimension_semantics=("arbitrary",), vmem_limit_bytes=VMEM_LIMIT),
        name="conv_prompt",
    )(u, w, b, lng, lnb)


def _pair_attention(q2, k2, v2, mask):
    m_rows = q2.shape[0]
    lane = lax.broadcasted_iota(jnp.int32, q2.shape, 1)
    zero = jnp.zeros_like(q2)
    qq = jnp.concatenate([jnp.where(lane < HEAD_DIM, q2, zero), jnp.where(lane >= HEAD_DIM, q2, zero)], axis=0)
    s = lax.dot_general(qq.astype(BF16), k2.astype(BF16), (((1,), (1,)), ((), ())), preferred_element_type=F32)
    s = jnp.where(mask, s, -jnp.inf)
    mx = jnp.max(s, axis=-1, keepdims=True)
    p = jnp.exp(s - mx)
    l = jnp.sum(p, axis=-1, keepdims=True)
    pv = jnp.dot(p.astype(BF16), v2.astype(BF16), preferred_element_type=F32)
    o = pv / l
    lse = jnp.broadcast_to(mx + jnp.log(l), o.shape)
    first = lane < HEAD_DIM
    return (jnp.where(first, o[:m_rows], o[m_rows:]),
            jnp.where(first, lse[:m_rows], lse[m_rows:]))


def _attn_prompt_body(d, q_ref, k_ref, v_ref, o_ref, lse_ref, qs, kbuf, vbuf, os_, ls_):
    tq = ATTN_TILE
    blk = ATTN_BLOCK
    span = blk * d
    i = pl.program_id(0)

    @pl.when(i == 0)
    def _():
        kbuf[:, 0:tq, :] = jnp.zeros((2, tq, LANES), F32)
        vbuf[:, 0:tq, :] = jnp.zeros((2, tq, LANES), F32)

    for hp in range(2):
        cols = slice(hp * LANES, (hp + 1) * LANES)
        qs[hp] = q_ref[:, cols] * SCALE
        kbuf[hp, tq:2 * tq, :] = k_ref[:, cols]
        vbuf[hp, tq:2 * tq, :] = v_ref[:, cols]

    row = lax.broadcasted_iota(jnp.int32, (2 * blk, 2 * blk), 0) & (blk - 1)
    col = lax.broadcasted_iota(jnp.int32, (2 * blk, 2 * blk), 1)
    band = (col >= row) & (col <= row + (N_KEYS - 1))

    def problem(j, carry):
        b = j // d
        r = j % d
        qstart = b * span + r
        kstart = tq + (b - 1) * span + r
        cmin = jnp.where((i == 0) & (b == 0), blk, 0)
        mask = band & (col >= cmin)
        for hp in range(2):
            if d == 1:
                qidx = pl.ds(pl.multiple_of(qstart, blk), blk)
                kidx = pl.ds(pl.multiple_of(kstart, blk), 2 * blk)
            else:
                qidx = pl.ds(qstart, blk, stride=d)
                kidx = pl.ds(kstart, 2 * blk, stride=d)
            o2, lse2 = _pair_attention(qs[hp, qidx, :], kbuf[hp, kidx, :], vbuf[hp, kidx, :], mask)
            os_[hp, qidx, :] = o2
            ls_[hp, qidx, :] = lse2
        return carry

    lax.fori_loop(0, tq // blk, problem, 0)

    for hp in range(2):
        cols = slice(hp * LANES, (hp + 1) * LANES)
        o_ref[:, cols] = os_[hp]
        lse_ref[:, cols] = ls_[hp]
    kbuf[:, 0:tq, :] = kbuf[:, tq:2 * tq, :]
    vbuf[:, 0:tq, :] = vbuf[:, tq:2 * tq, :]


def _attn_prompt(qkv, g, d):
    rows = qkv.shape[0]
    tq = ATTN_TILE
    n_grp = len(ATTN_GROUPS)
    slab = pltpu.VMEM((2, tq, LANES), F32)
    slab2 = pltpu.VMEM((2, 2 * tq, LANES), F32)
    return pl.pallas_call(
        functools.partial(_attn_prompt_body, d),
        grid=(rows // tq,),
        in_specs=[
            pl.BlockSpec((tq, D_GROUP), lambda i: (i, g)),
            pl.BlockSpec((tq, D_GROUP), lambda i: (i, n_grp + g)),
            pl.BlockSpec((tq, D_GROUP), lambda i: (i, 2 * n_grp + g)),
        ],
        out_specs=[
            pl.BlockSpec((tq, D_GROUP), lambda i: (i, 0)),
            pl.BlockSpec((tq, D_GROUP), lambda i: (i, 0)),
        ],
        out_shape=[
            jax.ShapeDtypeStruct((rows, D_GROUP), F32),
            jax.ShapeDtypeStruct((rows, D_GROUP), F32),
        ],
        scratch_shapes=[slab, slab2, slab2, slab, slab],
        compiler_params=pltpu.CompilerParams(
            dimension_semantics=("arbitrary",), vmem_limit_bytes=VMEM_LIMIT),
        name=f"attn_prompt_d{d}",
    )(qkv, qkv, qkv)


def _mixer_sample_body(t_new, hist_lens, *refs):
    n_grp = len(ATTN_GROUPS)
    (u_ref, qkv_ref, hist_ref, w_ref, b_ref, lng_ref, lnb_ref) = refs[:7]
    cache_refs = refs[7:7 + 2 * n_grp]
    pos = 7 + 2 * n_grp
    c_ref, state_ref = refs[pos:pos + 2]
    ol_refs = refs[pos + 2:pos + 2 + 2 * n_grp]
    new_cache_refs = refs[pos + 2 + 2 * n_grp:pos + 2 + 4 * n_grp]
    xs = refs[pos + 2 + 4 * n_grp]
    kv_bufs = refs[pos + 3 + 4 * n_grp:]

    n_hist = CONV_WIDTH - 1
    xs[0:n_hist, :] = hist_ref[...]
    xs[n_hist:n_hist + t_new, :] = u_ref[...]
    acc = jnp.broadcast_to(b_ref[...], (t_new, D_CONV))
    for k in range(CONV_WIDTH):
        acc = acc + w_ref[k:k + 1, :] * xs[k:k + t_new, :]
    c_ref[...] = _ln_swish(acc, lng_ref[...], lnb_ref[...])
    state_ref[...] = xs[t_new:t_new + n_hist, :]

    for g, (_, d) in enumerate(ATTN_GROUPS):
        hl = hist_lens[g]
        kx, vx = kv_bufs[2 * g], kv_bufs[2 * g + 1]
        n_keys_pad = kx.shape[0]

        @pl.when(pl.program_id(0) == 0)
        def _():
            kx[hl:n_keys_pad, :] = jnp.zeros((n_keys_pad - hl, D_GROUP), F32)
            vx[hl:n_keys_pad, :] = jnp.zeros((n_keys_pad - hl, D_GROUP), F32)

        kx[0:hl, :] = cache_refs[2 * g][...]
        vx[0:hl, :] = cache_refs[2 * g + 1][...]
        kx[hl:hl + t_new, :] = qkv_ref[:, D_ATTN + g * D_GROUP:D_ATTN + (g + 1) * D_GROUP]
        vx[hl:hl + t_new, :] = qkv_ref[:, 2 * D_ATTN + g * D_GROUP:2 * D_ATTN + (g + 1) * D_GROUP]
        new_cache_refs[2 * g][...] = kx[t_new:t_new + hl, :]
        new_cache_refs[2 * g + 1][...] = vx[t_new:t_new + hl, :]

        t_idx = lax.broadcasted_iota(jnp.int32, (2 * t_new, n_keys_pad), 0) & (t_new - 1)
        c_idx = lax.broadcasted_iota(jnp.int32, (2 * t_new, n_keys_pad), 1)
        delta = hl + t_idx - c_idx
        mask = (delta >= 0) & ((delta & (d - 1)) == 0) & (delta <= (N_KEYS - 1) * d)
        for hp in range(2):
            cols = slice(hp * LANES, (hp + 1) * LANES)
            q2 = qkv_ref[:, g * D_GROUP + hp * LANES:g * D_GROUP + (hp + 1) * LANES] * SCALE
            o2, lse2 = _pair_attention(q2, kx[:, cols], vx[:, cols], mask)
            ol_refs[2 * g][:, cols] = o2
            ol_refs[2 * g + 1][:, cols] = lse2


def _mixer_sample(layer, u, qkv, state_conv, caches, prev_new_caches, w, b, lng, lnb):
    n_grp = len(ATTN_GROUPS)
    depth, n_seq = caches[0].shape[:2]
    t_new = u.shape[0] // n_seq
    hist_lens = tuple(c.shape[2] for c in caches[0::2])
    n_hist = CONV_WIDTH - 1

    in_specs = [
        pl.BlockSpec((t_new, D_CONV), lambda n: (n, 0)),
        pl.BlockSpec((t_new, 3 * D_ATTN), lambda n: (n, 0)),
        pl.BlockSpec((None, None, n_hist, D_CONV), lambda n: (layer, n, 0, 0)),
        _const_spec((CONV_WIDTH, D_CONV)),
        _const_spec((1, D_CONV)),
        _const_spec((1, D_CONV)),
        _const_spec((1, D_CONV)),
    ]
    for g in range(n_grp):
        for _ in range(2):
            in_specs.append(pl.BlockSpec((None, None, hist_lens[g], D_GROUP), lambda n: (layer, n, 0, 0)))
    args = [u, qkv, state_conv, w, b, lng, lnb, *caches]
    aliases = {}
    n_fixed_out = 2 + 2 * n_grp
    if prev_new_caches is not None:
        for j, prev in enumerate(prev_new_caches):
            in_specs.append(pl.BlockSpec(memory_space=pl.ANY))
            aliases[len(args)] = n_fixed_out + j
            args.append(prev)

    out_specs = [
        pl.BlockSpec((t_new, D_CONV), lambda n: (n, 0)),
        pl.BlockSpec((None, n_hist, D_CONV), lambda n: (n, 0, 0)),
    ]
    out_shape = [
        jax.ShapeDtypeStruct((n_seq * t_new, D_CONV), F32),
        jax.ShapeDtypeStruct((n_seq, n_hist, D_CONV), F32),
    ]
    for g in range(n_grp):
        for _ in range(2):
            out_specs.append(pl.BlockSpec((t_new, D_GROUP), lambda n: (n, 0)))
            out_shape.append(jax.ShapeDtypeStruct((n_seq * t_new, D_GROUP), F32))
    for g in range(n_grp):
        for _ in range(2):
            out_specs.append(pl.BlockSpec((None, None, hist_lens[g], D_GROUP), lambda n: (layer, n, 0, 0)))
            out_shape.append(jax.ShapeDtypeStruct((depth, n_seq, hist_lens[g], D_GROUP), F32))

    scratch = [pltpu.VMEM((n_hist + t_new + 2, D_CONV), F32)]
    for g in range(n_grp):
        for _ in range(2):
            scratch.append(pltpu.VMEM((hist_lens[g] + LANES, D_GROUP), F32))

    def body(*refs):
        if prev_new_caches is not None:
            k = len(prev_new_caches)
            n_in = 7 + 2 * n_grp
            refs = refs[:n_in] + refs[n_in + k:]
        _mixer_sample_body(t_new, hist_lens, *refs)

    outs = pl.pallas_call(
        body,
        grid=(n_seq,),
        in_specs=in_specs,
        out_specs=out_specs,
        out_shape=out_shape,
        scratch_shapes=scratch,
        input_output_aliases=aliases,
        compiler_params=pltpu.CompilerParams(
            dimension_semantics=("arbitrary",), vmem_limit_bytes=VMEM_LIMIT),
        name=f"mixer_sample_l{layer}",
    )(*args)
    c, state = outs[0], outs[1]
    ol = outs[2:2 + 2 * n_grp]
    new_caches = outs[2 + 2 * n_grp:]
    return c, state, ol, new_caches


FF_CHUNK = 512


def _post_body(tm, per_seq, final, *refs):
    n_grp = len(ATTN_GROUPS)
    x_ref, c_ref = refs[0:2]
    ol_refs = refs[2:2 + 2 * n_grp]
    pos = 2 + 2 * n_grp
    sg_ref = refs[pos]
    (wco_ref, wao_ref, wo_ref, g2_ref, wup_ref, fw_ref, fb_ref, wdn_ref) = refs[pos + 1:pos + 9]
    pos += 9
    if final:
        gf_ref = refs[pos]
        pos += 1
    if per_seq:
        h0_ref, h1_ref = refs[pos:pos + 2]
        pos += 2
    y_ref, up_ref = refs[pos:pos + 2]
    ubuf = refs[pos + 2]
    carry = None if per_seq else refs[pos + 3]

    lses = [ol_refs[2 * g + 1][...] for g in range(n_grp)]
    lmax = functools.reduce(jnp.maximum, lses)
    es = [jnp.exp(l - lmax) for l in lses]
    den = functools.reduce(lambda a, b: a + b, es)
    attn = functools.reduce(lambda a, b: a + b, [(es[g] / den) * ol_refs[2 * g][...] for g in range(n_grp)])

    branch_conv = jnp.dot(c_ref[...].astype(BF16), wco_ref[...], preferred_element_type=F32)
    branch_attn = jnp.dot(attn.astype(BF16), wao_ref[...], preferred_element_type=F32)
    mixed = sg_ref[:, 0:D_MODEL] * branch_conv + sg_ref[:, D_MODEL:2 * D_MODEL] * branch_attn
    x1 = x_ref[...] + jnp.dot(mixed.astype(BF16), wo_ref[...], preferred_element_type=F32)

    h = _rms(x1, g2_ref[...]).astype(BF16)
    if not per_seq:
        @pl.when(pl.program_id(0) == 0)
        def _():
            carry[...] = jnp.zeros(carry.shape, F32)
    out_rows = up_ref.shape[0]
    t_in_seq = lax.broadcasted_iota(jnp.int32, (tm, FF_CHUNK), 0) & (SUBLANES - 1)

    def conv3(slot, c0):
        cols = slice(c0, c0 + FF_CHUNK)
        up = jnp.dot(h, wup_ref[:, cols], preferred_element_type=F32)
        up_ref[:, cols] = up[tm - out_rows:, :]
        ubuf[slot, SUBLANES:SUBLANES + tm, :] = up
        if per_seq:
            p1 = jnp.where(t_in_seq == 0, h1_ref[:, cols], ubuf[slot, SUBLANES - 1:SUBLANES - 1 + tm, :])
            p2 = jnp.where(t_in_seq == 0, h0_ref[:, cols],
                           jnp.where(t_in_seq == 1, h1_ref[:, cols], ubuf[slot, SUBLANES - 2:SUBLANES - 2 + tm, :]))
        else:
            ubuf[slot, 0:SUBLANES, :] = carry[:, cols]
            carry[:, cols] = up[tm - SUBLANES:, :]
            p1 = ubuf[slot, SUBLANES - 1:SUBLANES - 1 + tm, :]
            p2 = ubuf[slot, SUBLANES - 2:SUBLANES - 2 + tm, :]
        return fw_ref[0:1, cols] * p2 + fw_ref[1:2, cols] * p1 + fw_ref[2:3, cols] * up + fb_ref[:, cols]

    acc = jnp.zeros((tm, D_MODEL), F32)
    for c0 in range(0, D_FF, FF_CHUNK):
        val = conv3(0, c0)
        gate = conv3(1, D_FF + c0)
        act = (jax.nn.silu(gate) * val).astype(BF16)
        acc = acc + jnp.dot(act, wdn_ref[c0:c0 + FF_CHUNK, :], preferred_element_type=F32)
    x2 = x1 + acc
    if final:
        x2 = _rms(x2, gf_ref[...])
    y_ref[...] = x2


def _post(x, c, ol, sg, wco, wao, wo, g2, wup, fw, fb, wdn, gf, hist, tm):
    rows = x.shape[0]
    n_grp = len(ATTN_GROUPS)
    per_seq = hist is not None
    final = gf is not None
    row_spec = lambda w: pl.BlockSpec((tm, w), lambda i: (i, 0))
    in_specs = [row_spec(D_MODEL), row_spec(D_CONV)] + [row_spec(D_GROUP)] * (2 * n_grp) + [row_spec(2 * D_MODEL)]
    args = [x, c, *ol, sg]
    for a in (wco, wao, wo, g2, wup, fw, fb, wdn):
        in_specs.append(_const_spec(a.shape))
        args.append(a)
    if final:
        in_specs.append(_const_spec(gf.shape))
        args.append(gf)
    if per_seq:
        in_specs += [row_spec(2 * D_FF)] * 2
        args += list(hist)
    up_rows = rows if per_seq else SUBLANES
    up_spec = row_spec(2 * D_FF) if per_seq else pl.BlockSpec((SUBLANES, 2 * D_FF), lambda i: (0, 0))
    scratch = [pltpu.VMEM((2, SUBLANES + tm, FF_CHUNK), F32)]
    if not per_seq:
        scratch.append(pltpu.VMEM((SUBLANES, 2 * D_FF), F32))
    return pl.pallas_call(
        functools.partial(_post_body, tm, per_seq, final),
        grid=(rows // tm,),
        in_specs=in_specs,
        out_specs=[row_spec(D_MODEL), up_spec],
        out_shape=[
            jax.ShapeDtypeStruct((rows, D_MODEL), F32),
            jax.ShapeDtypeStruct((up_rows, 2 * D_FF), F32),
        ],
        scratch_shapes=scratch,
        compiler_params=pltpu.CompilerParams(
            dimension_semantics=("arbitrary",), vmem_limit_bytes=VMEM_LIMIT),
        name="post_sample" if per_seq else "post_prompt",
    )(*args)


def kernel(x_prompt, x_sample, state_conv, cache_k_w128, cache_v_w128, cache_k_w512, cache_v_w512, cache_k_w2048, cache_v_w2048, state_ffn_conv, norm_attn_g, w_in, conv_dw_w, conv_dw_b, conv_ln_g, conv_ln_b, w_conv_out, w_attn_out, w_out, norm_ffn_g, w_up, ffn_dw_w, ffn_dw_b, w_down, norm_final_g):
    depth = w_in.shape[0]
    n_b, seq, _ = x_prompt.shape
    n_s, t_new, _ = x_sample.shape
    assert n_b == 1 and seq % ATTN_TILE == 0 and t_new == SUBLANES
    n_grp = len(ATTN_GROUPS)
    caches = [c.reshape(c.shape[:3] + (D_GROUP,)) for c in
              (cache_k_w128, cache_v_w128, cache_k_w512, cache_v_w512, cache_k_w2048, cache_v_w2048)]

    xp = x_prompt.reshape(seq, D_MODEL)
    xs = x_sample.reshape(n_s * t_new, D_MODEL)
    row = lambda a: a.reshape(1, -1)
    gf = row(norm_final_g)

    conv_p, conv_s, ffn_p, ffn_s = [], [], [], []
    kv_p = [[[], []] for _ in range(n_grp)]
    new_caches = None
    for l in range(depth):
        last = l == depth - 1
        w_in_l = w_in[l].astype(BF16)
        wco, wao, wo = w_conv_out[l].astype(BF16), w_attn_out[l].astype(BF16), w_out[l].astype(BF16)
        wup, wdn = w_up[l].astype(BF16), w_down[l].astype(BF16)
        g1, g2 = row(norm_attn_g[l]), row(norm_ffn_g[l])
        cw, cb, lng, lnb = conv_dw_w[l], row(conv_dw_b[l]), row(conv_ln_g[l]), row(conv_ln_b[l])
        fw, fb = ffn_dw_w[l], row(ffn_dw_b[l])
        post_w = (wco, wao, wo, g2, wup, fw, fb, wdn, gf if last else None)

        u, qkv, sg = _inproj(xp, g1, w_in_l, 512)
        c = _conv_prompt(u, cw, cb, lng, lnb, 512)
        ol = []
        for g, (_, d) in enumerate(ATTN_GROUPS):
            ol += _attn_prompt(qkv, g, d)
        xp, up_tail = _post(xp, c, ol, sg, *post_w, None, 256)
        conv_p.append(u[seq - (CONV_WIDTH - 1):].reshape(1, CONV_WIDTH - 1, D_CONV))
        ffn_p.append(up_tail[SUBLANES - (FFN_CONV_WIDTH - 1):].reshape(1, FFN_CONV_WIDTH - 1, 2 * D_FF))
        for g, (window, _) in enumerate(ATTN_GROUPS):
            keep = min(window, seq)
            for j in range(2):
                c0 = (1 + j) * D_ATTN + g * D_GROUP
                kv_p[g][j].append(qkv[seq - keep:, c0:c0 + D_GROUP].reshape(1, keep, HEADS_PER_GROUP, HEAD_DIM))

        us, qkvs, sgs = _inproj(xs, g1, w_in_l, n_s * t_new)
        cs, state_s, ols, new_caches = _mixer_sample(l, us, qkvs, state_conv, caches, new_caches, cw, cb, lng, lnb)
        hist = tuple(jnp.repeat(state_ffn_conv[l][:, j], t_new, axis=0) for j in range(FFN_CONV_WIDTH - 1))
        xs, up_s = _post(xs, cs, ols, sgs, *post_w, hist, n_s * t_new)
        conv_s.append(state_s)
        ffn_s.append(up_s.reshape(n_s, t_new, 2 * D_FF)[:, t_new - (FFN_CONV_WIDTH - 1):])

    st = lambda lst: jnp.stack(lst, axis=0)
    kv_s = [nc.reshape(nc.shape[:3] + (HEADS_PER_GROUP, HEAD_DIM)) for nc in new_caches]
    return (xp.reshape(n_b, seq, D_MODEL), xs.reshape(n_s, t_new, D_MODEL),
            st(conv_p), st(conv_s),
            st(kv_p[0][0]), st(kv_p[0][1]), kv_s[0], kv_s[1],
            st(kv_p[1][0]), st(kv_p[1][1]), kv_s[2], kv_s[3],
            st(kv_p[2][0]), st(kv_p[2][1]), kv_s[4], kv_s[5],
            st(ffn_p), st(ffn_s))
```

```python
import functools

import jax
import jax.numpy as jnp
from jax import lax
from jax.experimental import pallas as pl
from jax.experimental.pallas import tpu as pltpu

F32 = jnp.float32
BF16 = jnp.bfloat16

D_MODEL = 1024
D_CONV = 1024
CONV_WIDTH = 31
HEAD_DIM = 64
HEADS_PER_GROUP = 4
ATTN_GROUPS = ((128, 1), (512, 4), (2048, 16))
N_KEYS = 129
D_GROUP = HEADS_PER_GROUP * HEAD_DIM
D_ATTN = len(ATTN_GROUPS) * D_GROUP
D_FF = 3 * D_MODEL
FFN_CONV_WIDTH = 3
EPS = 1e-6
SCALE = HEAD_DIM ** -0.5

LANES = 128
SUBLANES = 8
VMEM_LIMIT = 56 * 1024 * 1024

CONV_HALO = 32
ATTN_TILE = 2048
ATTN_BLOCK = 128


def _const_spec(shape):
    nd = len(shape)
    return pl.BlockSpec(shape, lambda *_: (0,) * nd, pipeline_mode=pl.Buffered(1))


def _rms(x, g):
    return x * lax.rsqrt(jnp.mean(x * x, axis=-1, keepdims=True) + EPS) * g


PROJ_CHUNK = 512


def _glu_project(h, w_ref, u_ref, row0):
    rows = h.shape[0]
    for j in range(0, D_CONV, PROJ_CHUNK):
        a = jnp.dot(h, w_ref[:, j:j + PROJ_CHUNK], preferred_element_type=F32)
        b = jnp.dot(h, w_ref[:, D_CONV + j:D_CONV + j + PROJ_CHUNK], preferred_element_type=F32)
        u_ref[row0:row0 + rows, j:j + PROJ_CHUNK] = a * jax.nn.sigmoid(b)


def _qkv_gate_tasks(h, w_ref, qkv_ref, sg_ref):
    tasks = []
    q0 = 2 * D_CONV
    g0 = q0 + 3 * D_ATTN

    def qkv_task(j, width):
        def run():
            qkv_ref[:, j:j + width] = jnp.dot(h, w_ref[:, q0 + j:q0 + j + width], preferred_element_type=F32)
        return run

    def gate_task(j):
        def run():
            sg_ref[:, j:j + PROJ_CHUNK] = jax.nn.sigmoid(
                jnp.dot(h, w_ref[:, g0 + j:g0 + j + PROJ_CHUNK], preferred_element_type=F32))
        return run

    for j in range(0, 3 * D_ATTN, PROJ_CHUNK):
        tasks.append(qkv_task(j, min(PROJ_CHUNK, 3 * D_ATTN - j)))
    for j in range(0, 2 * D_MODEL, PROJ_CHUNK):
        tasks.append(gate_task(j))
    return tasks


def _inproj_body(x_ref, g_ref, w_ref, u_ref, qkv_ref, sg_ref):
    h = _rms(x_ref[...], g_ref[...]).astype(BF16)
    _glu_project(h, w_ref, u_ref, 0)
    for task in _qkv_gate_tasks(h, w_ref, qkv_ref, sg_ref):
        task()


def _inproj(x, g, w_in, tm):
    rows = x.shape[0]
    d_in = w_in.shape[1]
    return pl.pallas_call(
        _inproj_body,
        grid=(rows // tm,),
        in_specs=[
            pl.BlockSpec((tm, D_MODEL), lambda i: (i, 0)),
            _const_spec((1, D_MODEL)),
            _const_spec((D_MODEL, d_in)),
        ],
        out_specs=[
            pl.BlockSpec((tm, D_CONV), lambda i: (i, 0)),
            pl.BlockSpec((tm, 3 * D_ATTN), lambda i: (i, 0)),
            pl.BlockSpec((tm, 2 * D_MODEL), lambda i: (i, 0)),
        ],
        out_shape=[
            jax.ShapeDtypeStruct((rows, D_CONV), F32),
            jax.ShapeDtypeStruct((rows, 3 * D_ATTN), F32),
            jax.ShapeDtypeStruct((rows, 2 * D_MODEL), F32),
        ],
        compiler_params=pltpu.CompilerParams(
            dimension_semantics=("parallel",), vmem_limit_bytes=VMEM_LIMIT),
        name="inproj",
    )(x, g, w_in)


def _ln_swish(y, g, b):
    mu = jnp.mean(y, axis=-1, keepdims=True)
    yc = y - mu
    var = jnp.mean(yc * yc, axis=-1, keepdims=True)
    z = yc * lax.rsqrt(var + EPS) * g + b
    return z * jax.nn.sigmoid(z)


def _shifted_rows(blk, s):
    if s == 0:
        return blk[:-1]
    rot = pltpu.roll(blk, SUBLANES - s, axis=1)
    sub = lax.broadcasted_iota(jnp.int32, rot[:-1].shape, 1)
    return jnp.where(sub < SUBLANES - s, rot[:-1], rot[1:])


def _conv31_lanes(buf, w8_ref, b_ref, ybuf, rows, l0):
    first = CONV_HALO - (CONV_WIDTH - 1)
    rc = 64
    nv = rc // SUBLANES
    lanes = slice(l0, l0 + LANES)

    def row_chunk(i, carry):
        r0 = pl.multiple_of(i * rc, rc)
        blk = buf[pl.ds(r0, rc + CONV_HALO + SUBLANES), lanes].reshape(nv + 5, SUBLANES, LANES)
        acc = jnp.broadcast_to(b_ref[:, lanes], (nv, SUBLANES, LANES))
        for s in range(SUBLANES):
            sh = _shifted_rows(blk, s)
            for a in range(5):
                k = SUBLANES * a + s - first
                if 0 <= k < CONV_WIDTH:
                    acc = acc + w8_ref[k, :, lanes] * sh[a:a + nv]
        ybuf[pl.ds(r0, rc), lanes] = acc.reshape(rc, LANES)
        return carry

    lax.fori_loop(0, rows // rc, row_chunk, 0)


def _conv_prompt_body(tm, u_ref, cw8_ref, cb_ref, lng_ref, lnb_ref, c_ref, ubuf, ybuf):
    @pl.when(pl.program_id(0) == 0)
    def _():
        ubuf[0:CONV_HALO, :] = jnp.zeros((CONV_HALO, D_CONV), F32)
        ubuf[CONV_HALO + tm:, :] = jnp.zeros((SUBLANES, D_CONV), F32)

    ubuf[CONV_HALO:CONV_HALO + tm, :] = u_ref[...]
    for l0 in range(0, D_CONV, LANES):
        _conv31_lanes(ubuf, cw8_ref, cb_ref, ybuf, tm, l0)
    c_ref[...] = _ln_swish(ybuf[...], lng_ref[...], lnb_ref[...]).astype(c_ref.dtype)
    ubuf[0:CONV_HALO, :] = ubuf[tm:tm + CONV_HALO, :]


def _conv_prompt(u, cw8, cb, lng, lnb, tm):
    rows = u.shape[0]
    return pl.pallas_call(
        functools.partial(_conv_prompt_body, tm),
        grid=(rows // tm,),
        in_specs=[
            pl.BlockSpec((tm, D_CONV), lambda i: (i, 0)),
            _const_spec((CONV_WIDTH, SUBLANES, D_CONV)),
            _const_spec((1, D_CONV)),
            _const_spec((1, D_CONV)),
            _const_spec((1, D_CONV)),
        ],
        out_specs=pl.BlockSpec((tm, D_CONV), lambda i: (i, 0)),
        out_shape=jax.ShapeDtypeStruct((rows, D_CONV), BF16),
        scratch_shapes=[
            pltpu.VMEM((CONV_HALO + tm + SUBLANES, D_CONV), F32),
            pltpu.VMEM((tm, D_CONV), F32),
        ],
        compiler_params=pltpu.CompilerParams(
            dimension_semantics=("arbitrary",), vmem_limit_bytes=VMEM_LIMIT),
        name="conv_prompt",
    )(u, cw8, cb, lng, lnb)


def _pair_attention(q2, k2, v2, mask):
    m_rows = q2.shape[0]
    lane = lax.broadcasted_iota(jnp.int32, q2.shape, 1)
    zero = jnp.zeros_like(q2)
    qq = jnp.concatenate([jnp.where(lane < HEAD_DIM, q2, zero), jnp.where(lane >= HEAD_DIM, q2, zero)], axis=0)
    s = lax.dot_general(qq.astype(BF16), k2.astype(BF16), (((1,), (1,)), ((), ())), preferred_element_type=F32)
    s = jnp.where(mask, s, -jnp.inf)
    mx = jnp.max(s, axis=-1, keepdims=True)
    p = jnp.exp(s - mx)
    l = jnp.sum(p, axis=-1, keepdims=True)
    pv = jnp.dot(p.astype(BF16), v2.astype(BF16), preferred_element_type=F32)
    o = pv / l
    lse = jnp.broadcast_to(mx + jnp.log(l), o.shape)
    first = lane < HEAD_DIM
    return (jnp.where(first, o[:m_rows], o[m_rows:]),
            jnp.where(first, lse[:m_rows], lse[m_rows:]))


def _attn_prompt_body(d, q_ref, k_ref, v_ref, o_ref, lse_ref, qs, kbuf, vbuf, os_, ls_):
    tq = ATTN_TILE
    blk = ATTN_BLOCK
    span = blk * d
    i = pl.program_id(0)

    @pl.when(i == 0)
    def _():
        kbuf[:, 0:tq, :] = jnp.zeros((2, tq, LANES), F32)
        vbuf[:, 0:tq, :] = jnp.zeros((2, tq, LANES), F32)

    for hp in range(2):
        cols = slice(hp * LANES, (hp + 1) * LANES)
        qs[hp] = q_ref[:, cols] * SCALE
        kbuf[hp, tq:2 * tq, :] = k_ref[:, cols]
        vbuf[hp, tq:2 * tq, :] = v_ref[:, cols]

    row = lax.broadcasted_iota(jnp.int32, (2 * blk, 2 * blk), 0) & (blk - 1)
    col = lax.broadcasted_iota(jnp.int32, (2 * blk, 2 * blk), 1)
    band = (col >= row) & (col <= row + (N_KEYS - 1))

    def problem(j, carry):
        b = j // d
        r = j % d
        qstart = b * span + r
        kstart = tq + (b - 1) * span + r
        cmin = jnp.where((i == 0) & (b == 0), blk, 0)
        mask = band & (col >= cmin)
        for hp in range(2):
            if d == 1:
                qidx = pl.ds(pl.multiple_of(qstart, blk), blk)
                kidx = pl.ds(pl.multiple_of(kstart, blk), 2 * blk)
            else:
                qidx = pl.ds(qstart, blk, stride=d)
                kidx = pl.ds(kstart, 2 * blk, stride=d)
            o2, lse2 = _pair_attention(qs[hp, qidx, :], kbuf[hp, kidx, :], vbuf[hp, kidx, :], mask)
            os_[hp, qidx, :] = o2
            ls_[hp, qidx, :] = lse2
        return carry

    lax.fori_loop(0, tq // blk, problem, 0, unroll=2)

    for hp in range(2):
        cols = slice(hp * LANES, (hp + 1) * LANES)
        o_ref[:, cols] = os_[hp]
        lse_ref[:, cols] = ls_[hp]
    kbuf[:, 0:tq, :] = kbuf[:, tq:2 * tq, :]
    vbuf[:, 0:tq, :] = vbuf[:, tq:2 * tq, :]


def _attn_prompt(qkv, g, d):
    rows = qkv.shape[0]
    tq = ATTN_TILE
    n_grp = len(ATTN_GROUPS)
    slab = pltpu.VMEM((2, tq, LANES), F32)
    slab2 = pltpu.VMEM((2, 2 * tq, LANES), F32)
    return pl.pallas_call(
        functools.partial(_attn_prompt_body, d),
        grid=(rows // tq,),
        in_specs=[
            pl.BlockSpec((tq, D_GROUP), lambda i: (i, g)),
            pl.BlockSpec((tq, D_GROUP), lambda i: (i, n_grp + g)),
            pl.BlockSpec((tq, D_GROUP), lambda i: (i, 2 * n_grp + g)),
        ],
        out_specs=[
            pl.BlockSpec((tq, D_GROUP), lambda i: (i, 0)),
            pl.BlockSpec((tq, D_GROUP), lambda i: (i, 0)),
        ],
        out_shape=[
            jax.ShapeDtypeStruct((rows, D_GROUP), F32),
            jax.ShapeDtypeStruct((rows, D_GROUP), F32),
        ],
        scratch_shapes=[slab, slab2, slab2, slab, slab],
        compiler_params=pltpu.CompilerParams(
            dimension_semantics=("arbitrary",), vmem_limit_bytes=VMEM_LIMIT),
        name=f"attn_prompt_d{d}",
    )(qkv, qkv, qkv)


def _mixer_sample_body(t_new, hist_lens, *refs):
    n_grp = len(ATTN_GROUPS)
    (u_ref, qkv_ref, hist_ref, w_ref, b_ref, lng_ref, lnb_ref) = refs[:7]
    cache_refs = refs[7:7 + 2 * n_grp]
    pos = 7 + 2 * n_grp
    c_ref, state_ref = refs[pos:pos + 2]
    ol_refs = refs[pos + 2:pos + 2 + 2 * n_grp]
    new_cache_refs = refs[pos + 2 + 2 * n_grp:pos + 2 + 4 * n_grp]
    xs = refs[pos + 2 + 4 * n_grp]
    kv_bufs = refs[pos + 3 + 4 * n_grp:]

    n_hist = CONV_WIDTH - 1
    xs[0:n_hist, :] = hist_ref[...]
    xs[n_hist:n_hist + t_new, :] = u_ref[...]
    acc = jnp.broadcast_to(b_ref[...], (t_new, D_CONV))
    for k in range(CONV_WIDTH):
        acc = acc + w_ref[k:k + 1, :] * xs[k:k + t_new, :]
    c_ref[...] = _ln_swish(acc, lng_ref[...], lnb_ref[...])
    state_ref[...] = xs[t_new:t_new + n_hist, :]

    for g, (_, d) in enumerate(ATTN_GROUPS):
        hl = hist_lens[g]
        kx, vx = kv_bufs[2 * g], kv_bufs[2 * g + 1]
        n_keys_pad = kx.shape[0]

        @pl.when(pl.program_id(0) == 0)
        def _():
            kx[hl:n_keys_pad, :] = jnp.zeros((n_keys_pad - hl, D_GROUP), F32)
            vx[hl:n_keys_pad, :] = jnp.zeros((n_keys_pad - hl, D_GROUP), F32)

        kx[0:hl, :] = cache_refs[2 * g][...]
        vx[0:hl, :] = cache_refs[2 * g + 1][...]
        kx[hl:hl + t_new, :] = qkv_ref[:, D_ATTN + g * D_GROUP:D_ATTN + (g + 1) * D_GROUP]
        vx[hl:hl + t_new, :] = qkv_ref[:, 2 * D_ATTN + g * D_GROUP:2 * D_ATTN + (g + 1) * D_GROUP]
        new_cache_refs[2 * g][...] = kx[t_new:t_new + hl, :]
        new_cache_refs[2 * g + 1][...] = vx[t_new:t_new + hl, :]

        t_idx = lax.broadcasted_iota(jnp.int32, (2 * t_new, n_keys_pad), 0) & (t_new - 1)
        c_idx = lax.broadcasted_iota(jnp.int32, (2 * t_new, n_keys_pad), 1)
        delta = hl + t_idx - c_idx
        mask = (delta >= 0) & ((delta & (d - 1)) == 0) & (delta <= (N_KEYS - 1) * d)
        for hp in range(2):
            cols = slice(hp * LANES, (hp + 1) * LANES)
            q2 = qkv_ref[:, g * D_GROUP + hp * LANES:g * D_GROUP + (hp + 1) * LANES] * SCALE
            o2, lse2 = _pair_attention(q2, kx[:, cols], vx[:, cols], mask)
            ol_refs[2 * g][:, cols] = o2
            ol_refs[2 * g + 1][:, cols] = lse2


def _mixer_sample(layer, u, qkv, state_conv, caches, prev_new_caches, w, b, lng, lnb):
    n_grp = len(ATTN_GROUPS)
    depth, n_seq = caches[0].shape[:2]
    t_new = u.shape[0] // n_seq
    hist_lens = tuple(c.shape[2] for c in caches[0::2])
    n_hist = CONV_WIDTH - 1

    in_specs = [
        pl.BlockSpec((t_new, D_CONV), lambda n: (n, 0)),
        pl.BlockSpec((t_new, 3 * D_ATTN), lambda n: (n, 0)),
        pl.BlockSpec((None, None, n_hist, D_CONV), lambda n: (layer, n, 0, 0)),
        _const_spec((CONV_WIDTH, D_CONV)),
        _const_spec((1, D_CONV)),
        _const_spec((1, D_CONV)),
        _const_spec((1, D_CONV)),
    ]
    for g in range(n_grp):
        for _ in range(2):
            in_specs.append(pl.BlockSpec((None, None, hist_lens[g], D_GROUP), lambda n: (layer, n, 0, 0)))
    args = [u, qkv, state_conv, w, b, lng, lnb, *caches]
    aliases = {}
    n_fixed_out = 2 + 2 * n_grp
    if prev_new_caches is not None:
        for j, prev in enumerate(prev_new_caches):
            in_specs.append(pl.BlockSpec(memory_space=pl.ANY))
            aliases[len(args)] = n_fixed_out + j
            args.append(prev)

    out_specs = [
        pl.BlockSpec((t_new, D_CONV), lambda n: (n, 0)),
        pl.BlockSpec((None, n_hist, D_CONV), lambda n: (n, 0, 0)),
    ]
    out_shape = [
        jax.ShapeDtypeStruct((n_seq * t_new, D_CONV), F32),
        jax.ShapeDtypeStruct((n_seq, n_hist, D_CONV), F32),
    ]
    for g in range(n_grp):
        for _ in range(2):
            out_specs.append(pl.BlockSpec((t_new, D_GROUP), lambda n: (n, 0)))
            out_shape.append(jax.ShapeDtypeStruct((n_seq * t_new, D_GROUP), F32))
    for g in range(n_grp):
        for _ in range(2):
            out_specs.append(pl.BlockSpec((None, None, hist_lens[g], D_GROUP), lambda n: (layer, n, 0, 0)))
            out_shape.append(jax.ShapeDtypeStruct((depth, n_seq, hist_lens[g], D_GROUP), F32))

    scratch = [pltpu.VMEM((n_hist + t_new + 2, D_CONV), F32)]
    for g in range(n_grp):
        for _ in range(2):
            scratch.append(pltpu.VMEM((hist_lens[g] + LANES, D_GROUP), F32))

    def body(*refs):
        if prev_new_caches is not None:
            k = len(prev_new_caches)
            n_in = 7 + 2 * n_grp
            refs = refs[:n_in] + refs[n_in + k:]
        _mixer_sample_body(t_new, hist_lens, *refs)

    outs = pl.pallas_call(
        body,
        grid=(n_seq,),
        in_specs=in_specs,
        out_specs=out_specs,
        out_shape=out_shape,
        scratch_shapes=scratch,
        input_output_aliases=aliases,
        compiler_params=pltpu.CompilerParams(
            dimension_semantics=("arbitrary",), vmem_limit_bytes=VMEM_LIMIT),
        name=f"mixer_sample_l{layer}",
    )(*args)
    c, state = outs[0], outs[1]
    ol = outs[2:2 + 2 * n_grp]
    new_caches = outs[2 + 2 * n_grp:]
    return c, state, ol, new_caches


FF_CHUNK = 512


def _post_body(tm, per_seq, final, *refs):
    n_grp = len(ATTN_GROUPS)
    x_ref, c_ref = refs[0:2]
    ol_refs = refs[2:2 + 2 * n_grp]
    pos = 2 + 2 * n_grp
    sg_ref = refs[pos]
    (wco_ref, wao_ref, wo_ref, g2_ref, wup_ref, fw_ref, fb_ref, wdn_ref) = refs[pos + 1:pos + 9]
    pos += 9
    if final:
        gf_ref = refs[pos]
        pos += 1
    if per_seq:
        h0_ref, h1_ref = refs[pos:pos + 2]
        pos += 2
    y_ref, up_ref = refs[pos:pos + 2]
    carry = None if per_seq else refs[pos + 2]

    lses = [ol_refs[2 * g + 1][...] for g in range(n_grp)]
    lmax = functools.reduce(jnp.maximum, lses)
    es = [jnp.exp(l - lmax) for l in lses]
    den = functools.reduce(lambda a, b: a + b, es)
    attn = functools.reduce(lambda a, b: a + b, [(es[g] / den) * ol_refs[2 * g][...] for g in range(n_grp)])

    branch_conv = jnp.dot(c_ref[...].astype(BF16), wco_ref[...], preferred_element_type=F32)
    branch_attn = jnp.dot(attn.astype(BF16), wao_ref[...], preferred_element_type=F32)
    mixed = sg_ref[:, 0:D_MODEL] * branch_conv + sg_ref[:, D_MODEL:2 * D_MODEL] * branch_attn
    x1 = x_ref[...] + jnp.dot(mixed.astype(BF16), wo_ref[...], preferred_element_type=F32)

    h = _rms(x1, g2_ref[...]).astype(BF16)
    if not per_seq:
        @pl.when(pl.program_id(0) == 0)
        def _():
            carry[...] = jnp.zeros(carry.shape, F32)
    out_rows = up_ref.shape[0]
    t_in_seq = lax.broadcasted_iota(jnp.int32, (tm, FF_CHUNK), 0) & (SUBLANES - 1)

    def conv3(c0):
        cols = slice(c0, c0 + FF_CHUNK)
        up = jnp.dot(h, wup_ref[:, cols], preferred_element_type=F32)
        up_ref[:, cols] = up[tm - out_rows:, :]
        head = up[0:SUBLANES, :] if per_seq else carry[:, cols]
        ext = jnp.concatenate([head, up], axis=0).reshape(tm // SUBLANES + 1, SUBLANES, FF_CHUNK)
        p1 = _shifted_rows(ext, SUBLANES - 1).reshape(tm, FF_CHUNK)
        p2 = _shifted_rows(ext, SUBLANES - 2).reshape(tm, FF_CHUNK)
        if per_seq:
            p1 = jnp.where(t_in_seq == 0, h1_ref[:, cols], p1)
            p2 = jnp.where(t_in_seq == 0, h0_ref[:, cols], jnp.where(t_in_seq == 1, h1_ref[:, cols], p2))
        else:
            carry[:, cols] = up[tm - SUBLANES:, :]
        return fw_ref[0:1, cols] * p2 + fw_ref[1:2, cols] * p1 + fw_ref[2:3, cols] * up + fb_ref[:, cols]

    acc = jnp.zeros((tm, D_MODEL), F32)
    for c0 in range(0, D_FF, FF_CHUNK):
        val = conv3(c0)
        gate = conv3(D_FF + c0)
        act = (jax.nn.silu(gate) * val).astype(BF16)
        acc = acc + jnp.dot(act, wdn_ref[c0:c0 + FF_CHUNK, :], preferred_element_type=F32)
    x2 = x1 + acc
    if final:
        x2 = _rms(x2, gf_ref[...])
    y_ref[...] = x2


def _post(x, c, ol, sg, wco, wao, wo, g2, wup, fw, fb, wdn, gf, hist, tm):
    rows = x.shape[0]
    n_grp = len(ATTN_GROUPS)
    per_seq = hist is not None
    final = gf is not None
    row_spec = lambda w: pl.BlockSpec((tm, w), lambda i: (i, 0))
    in_specs = [row_spec(D_MODEL), row_spec(D_CONV)] + [row_spec(D_GROUP)] * (2 * n_grp) + [row_spec(2 * D_MODEL)]
    args = [x, c, *ol, sg]
    for a in (wco, wao, wo, g2, wup, fw, fb, wdn):
        in_specs.append(_const_spec(a.shape))
        args.append(a)
    if final:
        in_specs.append(_const_spec(gf.shape))
        args.append(gf)
    if per_seq:
        in_specs += [row_spec(2 * D_FF)] * 2
        args += list(hist)
    up_rows = rows if per_seq else SUBLANES
    up_spec = row_spec(2 * D_FF) if per_seq else pl.BlockSpec((SUBLANES, 2 * D_FF), lambda i: (0, 0))
    scratch = [] if per_seq else [pltpu.VMEM((SUBLANES, 2 * D_FF), F32)]
    return pl.pallas_call(
        functools.partial(_post_body, tm, per_seq, final),
        grid=(rows // tm,),
        in_specs=in_specs,
        out_specs=[row_spec(D_MODEL), up_spec],
        out_shape=[
            jax.ShapeDtypeStruct((rows, D_MODEL), F32),
            jax.ShapeDtypeStruct((up_rows, 2 * D_FF), F32),
        ],
        scratch_shapes=scratch,
        compiler_params=pltpu.CompilerParams(
            dimension_semantics=("arbitrary",), vmem_limit_bytes=VMEM_LIMIT),
        name="post_sample" if per_seq else "post_prompt",
    )(*args)


def kernel(x_prompt, x_sample, state_conv, cache_k_w128, cache_v_w128, cache_k_w512, cache_v_w512, cache_k_w2048, cache_v_w2048, state_ffn_conv, norm_attn_g, w_in, conv_dw_w, conv_dw_b, conv_ln_g, conv_ln_b, w_conv_out, w_attn_out, w_out, norm_ffn_g, w_up, ffn_dw_w, ffn_dw_b, w_down, norm_final_g):
    depth = w_in.shape[0]
    n_b, seq, _ = x_prompt.shape
    n_s, t_new, _ = x_sample.shape
    assert n_b == 1 and seq % ATTN_TILE == 0 and t_new == SUBLANES
    n_grp = len(ATTN_GROUPS)
    caches = [c.reshape(c.shape[:3] + (D_GROUP,)) for c in
              (cache_k_w128, cache_v_w128, cache_k_w512, cache_v_w512, cache_k_w2048, cache_v_w2048)]

    xp = x_prompt.reshape(seq, D_MODEL)
    xs = x_sample.reshape(n_s * t_new, D_MODEL)
    row = lambda a: a.reshape(1, -1)
    gf = row(norm_final_g)

    conv_p, conv_s, ffn_p, ffn_s = [], [], [], []
    kv_p = [[[], []] for _ in range(n_grp)]
    new_caches = None
    for l in range(depth):
        last = l == depth - 1
        w_in_l = w_in[l].astype(BF16)
        wco, wao, wo = w_conv_out[l].astype(BF16), w_attn_out[l].astype(BF16), w_out[l].astype(BF16)
        wup, wdn = w_up[l].astype(BF16), w_down[l].astype(BF16)
        g1, g2 = row(norm_attn_g[l]), row(norm_ffn_g[l])
        cw, cb, lng, lnb = conv_dw_w[l], row(conv_dw_b[l]), row(conv_ln_g[l]), row(conv_ln_b[l])
        fw, fb = ffn_dw_w[l], row(ffn_dw_b[l])
        post_w = (wco, wao, wo, g2, wup, fw, fb, wdn, gf if last else None)

        cw8 = jnp.broadcast_to(cw[:, None, :], (CONV_WIDTH, SUBLANES, D_CONV))
        u, qkv, sg = _inproj(xp, g1, w_in_l, 512)
        c = _conv_prompt(u, cw8, cb, lng, lnb, 512)
        ol = []
        for g, (_, d) in enumerate(ATTN_GROUPS):
            ol += _attn_prompt(qkv, g, d)
        xp, up_tail = _post(xp, c, ol, sg, *post_w, None, 256)
        conv_p.append(u[seq - (CONV_WIDTH - 1):].reshape(1, CONV_WIDTH - 1, D_CONV))
        ffn_p.append(up_tail[SUBLANES - (FFN_CONV_WIDTH - 1):].reshape(1, FFN_CONV_WIDTH - 1, 2 * D_FF))
        for g, (window, _) in enumerate(ATTN_GROUPS):
            keep = min(window, seq)
            for j in range(2):
                c0 = (1 + j) * D_ATTN + g * D_GROUP
                kv_p[g][j].append(qkv[seq - keep:, c0:c0 + D_GROUP].reshape(1, keep, HEADS_PER_GROUP, HEAD_DIM))

        us, qkvs, sgs = _inproj(xs, g1, w_in_l, n_s * t_new)
        cs, state_s, ols, new_caches = _mixer_sample(l, us, qkvs, state_conv, caches, new_caches, cw, cb, lng, lnb)
        hist = tuple(jnp.repeat(state_ffn_conv[l][:, j], t_new, axis=0) for j in range(FFN_CONV_WIDTH - 1))
        xs, up_s = _post(xs, cs, ols, sgs, *post_w, hist, n_s * t_new)
        conv_s.append(state_s)
        ffn_s.append(up_s.reshape(n_s, t_new, 2 * D_FF)[:, t_new - (FFN_CONV_WIDTH - 1):])

    st = lambda lst: jnp.stack(lst, axis=0)
    kv_s = [nc.reshape(nc.shape[:3] + (HEADS_PER_GROUP, HEAD_DIM)) for nc in new_caches]
    return (xp.reshape(n_b, seq, D_MODEL), xs.reshape(n_s, t_new, D_MODEL),
            st(conv_p), st(conv_s),
            st(kv_p[0][0]), st(kv_p[0][1]), kv_s[0], kv_s[1],
            st(kv_p[1][0]), st(kv_p[1][1]), kv_s[2], kv_s[3],
            st(kv_p[2][0]), st(kv_p[2][1]), kv_s[4], kv_s[5],
            st(ffn_p), st(ffn_s))
```

```python
import functools

import jax
import jax.numpy as jnp
from jax import lax
from jax.experimental import pallas as pl
from jax.experimental.pallas import tpu as pltpu

F32 = jnp.float32
BF16 = jnp.bfloat16

D_MODEL = 1024
D_CONV = 1024
CONV_WIDTH = 31
HEAD_DIM = 64
HEADS_PER_GROUP = 4
ATTN_GROUPS = ((128, 1), (512, 4), (2048, 16))
N_KEYS = 129
D_GROUP = HEADS_PER_GROUP * HEAD_DIM
D_ATTN = len(ATTN_GROUPS) * D_GROUP
D_FF = 3 * D_MODEL
FFN_CONV_WIDTH = 3
EPS = 1e-6
SCALE = HEAD_DIM ** -0.5

LANES = 128
SUBLANES = 8
VMEM_LIMIT = 56 * 1024 * 1024

CONV_HALO = 32
ATTN_TILE = 2048
ATTN_BLOCK = 128


def _const_spec(shape):
    nd = len(shape)
    return pl.BlockSpec(shape, lambda *_: (0,) * nd, pipeline_mode=pl.Buffered(1))


def _rms(x, g):
    return x * lax.rsqrt(jnp.mean(x * x, axis=-1, keepdims=True) + EPS) * g


PROJ_CHUNK = 512


def _glu_project(h, w_ref, u_ref, row0):
    rows = h.shape[0]
    for j in range(0, D_CONV, PROJ_CHUNK):
        a = jnp.dot(h, w_ref[:, j:j + PROJ_CHUNK], preferred_element_type=F32)
        b = jnp.dot(h, w_ref[:, D_CONV + j:D_CONV + j + PROJ_CHUNK], preferred_element_type=F32)
        u_ref[row0:row0 + rows, j:j + PROJ_CHUNK] = a * jax.nn.sigmoid(b)


def _qkv_gate_tasks(h, w_ref, qkv_ref, sg_ref):
    tasks = []
    q0 = 2 * D_CONV
    g0 = q0 + 3 * D_ATTN

    def qkv_task(j, width):
        def run():
            qkv_ref[:, j:j + width] = jnp.dot(h, w_ref[:, q0 + j:q0 + j + width], preferred_element_type=F32)
        return run

    def gate_task(j):
        def run():
            sg_ref[:, j:j + PROJ_CHUNK] = jax.nn.sigmoid(
                jnp.dot(h, w_ref[:, g0 + j:g0 + j + PROJ_CHUNK], preferred_element_type=F32))
        return run

    for j in range(0, 3 * D_ATTN, PROJ_CHUNK):
        tasks.append(qkv_task(j, min(PROJ_CHUNK, 3 * D_ATTN - j)))
    for j in range(0, 2 * D_MODEL, PROJ_CHUNK):
        tasks.append(gate_task(j))
    return tasks


def _inproj_body(x_ref, g_ref, w_ref, u_ref, qkv_ref, sg_ref):
    h = _rms(x_ref[...], g_ref[...]).astype(BF16)
    _glu_project(h, w_ref, u_ref, 0)
    for task in _qkv_gate_tasks(h, w_ref, qkv_ref, sg_ref):
        task()


def _inproj(x, g, w_in, tm):
    rows = x.shape[0]
    d_in = w_in.shape[1]
    return pl.pallas_call(
        _inproj_body,
        grid=(rows // tm,),
        in_specs=[
            pl.BlockSpec((tm, D_MODEL), lambda i: (i, 0)),
            _const_spec((1, D_MODEL)),
            _const_spec((D_MODEL, d_in)),
        ],
        out_specs=[
            pl.BlockSpec((tm, D_CONV), lambda i: (i, 0)),
            pl.BlockSpec((tm, 3 * D_ATTN), lambda i: (i, 0)),
            pl.BlockSpec((tm, 2 * D_MODEL), lambda i: (i, 0)),
        ],
        out_shape=[
            jax.ShapeDtypeStruct((rows, D_CONV), F32),
            jax.ShapeDtypeStruct((rows, 3 * D_ATTN), F32),
            jax.ShapeDtypeStruct((rows, 2 * D_MODEL), F32),
        ],
        compiler_params=pltpu.CompilerParams(
            dimension_semantics=("parallel",), vmem_limit_bytes=VMEM_LIMIT),
        name="inproj",
    )(x, g, w_in)


def _ln_swish(y, g, b):
    mu = jnp.mean(y, axis=-1, keepdims=True)
    yc = y - mu
    var = jnp.mean(yc * yc, axis=-1, keepdims=True)
    z = yc * lax.rsqrt(var + EPS) * g + b
    return z * jax.nn.sigmoid(z)


def _shifted_rows(blk, s):
    if s == 0:
        return blk[:-1]
    rot = pltpu.roll(blk, SUBLANES - s, axis=1)
    sub = lax.broadcasted_iota(jnp.int32, rot[:-1].shape, 1)
    return jnp.where(sub < SUBLANES - s, rot[:-1], rot[1:])


def _conv31_lanes(buf, w8_ref, b_ref, ybuf, rows, l0):
    first = CONV_HALO - (CONV_WIDTH - 1)
    rc = 64
    nv = rc // SUBLANES
    lanes = slice(l0, l0 + LANES)

    def row_chunk(i, carry):
        r0 = pl.multiple_of(i * rc, rc)
        blk = buf[pl.ds(r0, rc + CONV_HALO + SUBLANES), lanes].reshape(nv + 5, SUBLANES, LANES)
        acc = jnp.broadcast_to(b_ref[:, lanes], (nv, SUBLANES, LANES))
        for s in range(SUBLANES):
            sh = _shifted_rows(blk, s)
            for a in range(5):
                k = SUBLANES * a + s - first
                if 0 <= k < CONV_WIDTH:
                    acc = acc + w8_ref[k, :, lanes] * sh[a:a + nv]
        ybuf[pl.ds(r0, rc), lanes] = acc.reshape(rc, LANES)
        return carry

    lax.fori_loop(0, rows // rc, row_chunk, 0)


def _conv_prompt_body(tm, u_ref, cw8_ref, cb_ref, lng_ref, lnb_ref, c_ref, ubuf, ybuf):
    @pl.when(pl.program_id(0) == 0)
    def _():
        ubuf[0:CONV_HALO, :] = jnp.zeros((CONV_HALO, D_CONV), F32)
        ubuf[CONV_HALO + tm:, :] = jnp.zeros((SUBLANES, D_CONV), F32)

    ubuf[CONV_HALO:CONV_HALO + tm, :] = u_ref[...]
    for l0 in range(0, D_CONV, LANES):
        _conv31_lanes(ubuf, cw8_ref, cb_ref, ybuf, tm, l0)
    c_ref[...] = _ln_swish(ybuf[...], lng_ref[...], lnb_ref[...]).astype(c_ref.dtype)
    ubuf[0:CONV_HALO, :] = ubuf[tm:tm + CONV_HALO, :]


def _conv_prompt(u, cw8, cb, lng, lnb, tm):
    rows = u.shape[0]
    return pl.pallas_call(
        functools.partial(_conv_prompt_body, tm),
        grid=(rows // tm,),
        in_specs=[
            pl.BlockSpec((tm, D_CONV), lambda i: (i, 0)),
            _const_spec((CONV_WIDTH, SUBLANES, D_CONV)),
            _const_spec((1, D_CONV)),
            _const_spec((1, D_CONV)),
            _const_spec((1, D_CONV)),
        ],
        out_specs=pl.BlockSpec((tm, D_CONV), lambda i: (i, 0)),
        out_shape=jax.ShapeDtypeStruct((rows, D_CONV), BF16),
        scratch_shapes=[
            pltpu.VMEM((CONV_HALO + tm + SUBLANES, D_CONV), F32),
            pltpu.VMEM((tm, D_CONV), F32),
        ],
        compiler_params=pltpu.CompilerParams(
            dimension_semantics=("arbitrary",), vmem_limit_bytes=VMEM_LIMIT),
        name="conv_prompt",
    )(u, cw8, cb, lng, lnb)


def _pair_attention(q2, k2, v2, mask):
    m_rows = q2.shape[0]
    lane = lax.broadcasted_iota(jnp.int32, q2.shape, 1)
    zero = jnp.zeros_like(q2)
    qq = jnp.concatenate([jnp.where(lane < HEAD_DIM, q2, zero), jnp.where(lane >= HEAD_DIM, q2, zero)], axis=0)
    s = lax.dot_general(qq.astype(BF16), k2.astype(BF16), (((1,), (1,)), ((), ())), preferred_element_type=F32)
    s = jnp.where(mask, s, -jnp.inf)
    mx = jnp.max(s, axis=-1, keepdims=True)
    p = jnp.exp(s - mx)
    l = jnp.sum(p, axis=-1, keepdims=True)
    pv = jnp.dot(p.astype(BF16), v2.astype(BF16), preferred_element_type=F32)
    o = pv / l
    lse = jnp.broadcast_to(mx + jnp.log(l), o.shape)
    first = lane < HEAD_DIM
    return (jnp.where(first, o[:m_rows], o[m_rows:]),
            jnp.where(first, lse[:m_rows], lse[m_rows:]))


def _pair_attention_cached(q2, kt, vt, kn, vn, mask_c, mask_n):
    m_rows = q2.shape[0]
    lane = lax.broadcasted_iota(jnp.int32, q2.shape, 1)
    zero = jnp.zeros_like(q2)
    qq = jnp.concatenate([jnp.where(lane < HEAD_DIM, q2, zero), jnp.where(lane >= HEAD_DIM, q2, zero)],
                         axis=0).astype(BF16)
    nt = (((1,), (1,)), ((), ()))
    s_c = jnp.where(mask_c, jnp.dot(qq, kt.astype(BF16), preferred_element_type=F32), -jnp.inf)
    s_n = jnp.where(mask_n, lax.dot_general(qq, kn.astype(BF16), nt, preferred_element_type=F32), -jnp.inf)
    mx = jnp.maximum(jnp.max(s_c, axis=-1, keepdims=True), jnp.max(s_n, axis=-1, keepdims=True))
    p_c = jnp.exp(s_c - mx)
    p_n = jnp.exp(s_n - mx)
    l = jnp.sum(p_c, axis=-1, keepdims=True) + jnp.sum(p_n, axis=-1, keepdims=True)
    pv = (lax.dot_general(p_c.astype(BF16), vt.astype(BF16), nt, preferred_element_type=F32)
          + jnp.dot(p_n.astype(BF16), vn.astype(BF16), preferred_element_type=F32))
    o = pv / l
    lse = jnp.broadcast_to(mx + jnp.log(l), o.shape)
    first = lane < HEAD_DIM
    return (jnp.where(first, o[:m_rows], o[m_rows:]),
            jnp.where(first, lse[:m_rows], lse[m_rows:]))


def _attn_prompt_body(d, q_ref, k_ref, v_ref, o_ref, lse_ref, qs, kbuf, vbuf, os_, ls_):
    tq = ATTN_TILE
    blk = ATTN_BLOCK
    span = blk * d
    i = pl.program_id(0)

    @pl.when(i == 0)
    def _():
        kbuf[:, 0:tq, :] = jnp.zeros((2, tq, LANES), F32)
        vbuf[:, 0:tq, :] = jnp.zeros((2, tq, LANES), F32)

    for hp in range(2):
        cols = slice(hp * LANES, (hp + 1) * LANES)
        qs[hp] = q_ref[:, cols] * SCALE
        kbuf[hp, tq:2 * tq, :] = k_ref[:, cols]
        vbuf[hp, tq:2 * tq, :] = v_ref[:, cols]

    row = lax.broadcasted_iota(jnp.int32, (2 * blk, 2 * blk), 0) & (blk - 1)
    col = lax.broadcasted_iota(jnp.int32, (2 * blk, 2 * blk), 1)
    band = (col >= row) & (col <= row + (N_KEYS - 1))

    def problem(j, carry):
        b = j // d
        r = j % d
        qstart = b * span + r
        kstart = tq + (b - 1) * span + r
        cmin = jnp.where((i == 0) & (b == 0), blk, 0)
        mask = band & (col >= cmin)
        for hp in range(2):
            if d == 1:
                qidx = pl.ds(pl.multiple_of(qstart, blk), blk)
                kidx = pl.ds(pl.multiple_of(kstart, blk), 2 * blk)
            else:
                qidx = pl.ds(qstart, blk, stride=d)
                kidx = pl.ds(kstart, 2 * blk, stride=d)
            o2, lse2 = _pair_attention(qs[hp, qidx, :], kbuf[hp, kidx, :], vbuf[hp, kidx, :], mask)
            os_[hp, qidx, :] = o2
            ls_[hp, qidx, :] = lse2
        return carry

    lax.fori_loop(0, tq // blk, problem, 0, unroll=4)

    for hp in range(2):
        cols = slice(hp * LANES, (hp + 1) * LANES)
        o_ref[:, cols] = os_[hp]
        lse_ref[:, cols] = ls_[hp]
    kbuf[:, 0:tq, :] = kbuf[:, tq:2 * tq, :]
    vbuf[:, 0:tq, :] = vbuf[:, tq:2 * tq, :]


def _attn_prompt(qkv, g, d):
    rows = qkv.shape[0]
    tq = ATTN_TILE
    n_grp = len(ATTN_GROUPS)
    slab = pltpu.VMEM((2, tq, LANES), F32)
    slab2 = pltpu.VMEM((2, 2 * tq, LANES), F32)
    return pl.pallas_call(
        functools.partial(_attn_prompt_body, d),
        grid=(rows // tq,),
        in_specs=[
            pl.BlockSpec((tq, D_GROUP), lambda i: (i, g)),
            pl.BlockSpec((tq, D_GROUP), lambda i: (i, n_grp + g)),
            pl.BlockSpec((tq, D_GROUP), lambda i: (i, 2 * n_grp + g)),
        ],
        out_specs=[
            pl.BlockSpec((tq, D_GROUP), lambda i: (i, 0)),
            pl.BlockSpec((tq, D_GROUP), lambda i: (i, 0)),
        ],
        out_shape=[
            jax.ShapeDtypeStruct((rows, D_GROUP), F32),
            jax.ShapeDtypeStruct((rows, D_GROUP), F32),
        ],
        scratch_shapes=[slab, slab2, slab2, slab, slab],
        compiler_params=pltpu.CompilerParams(
            dimension_semantics=("arbitrary",), vmem_limit_bytes=VMEM_LIMIT),
        name=f"attn_prompt_d{d}",
    )(qkv, qkv, qkv)


def _mixer_sample_body(t_new, hist_lens, *refs):
    n_grp = len(ATTN_GROUPS)
    (u_ref, qkv_ref, hist_ref, w_ref, b_ref, lng_ref, lnb_ref) = refs[:7]
    cache_refs = refs[7:7 + 2 * n_grp]
    pos = 7 + 2 * n_grp
    c_ref, state_ref = refs[pos:pos + 2]
    ol_refs = refs[pos + 2:pos + 2 + 2 * n_grp]
    new_cache_refs = refs[pos + 2 + 2 * n_grp:pos + 2 + 4 * n_grp]
    xs = refs[pos + 2 + 4 * n_grp]

    n_hist = CONV_WIDTH - 1
    xs[0:n_hist, :] = hist_ref[...]
    xs[n_hist:n_hist + t_new, :] = u_ref[...]
    acc = jnp.broadcast_to(b_ref[...], (t_new, D_CONV))
    for k in range(CONV_WIDTH):
        acc = acc + w_ref[k:k + 1, :] * xs[k:k + t_new, :]
    c_ref[...] = _ln_swish(acc, lng_ref[...], lnb_ref[...])
    state_ref[...] = xs[t_new:t_new + n_hist, :]

    def window_mask(delta, d):
        return (delta >= 0) & ((delta & (d - 1)) == 0) & (delta <= (N_KEYS - 1) * d)

    pad_rows = jnp.zeros((LANES - t_new, D_GROUP), F32)
    t_new_idx = lax.broadcasted_iota(jnp.int32, (2 * t_new, LANES), 0) & (t_new - 1)
    j_new_idx = lax.broadcasted_iota(jnp.int32, (2 * t_new, LANES), 1)
    lane = lax.broadcasted_iota(jnp.int32, (D_GROUP, LANES), 1)
    for g, (_, d) in enumerate(ATTN_GROUPS):
        hl = hist_lens[g]
        kt_ref, vt_ref = cache_refs[2 * g], cache_refs[2 * g + 1]
        kn = jnp.concatenate([qkv_ref[:, D_ATTN + g * D_GROUP:D_ATTN + (g + 1) * D_GROUP], pad_rows], axis=0)
        vn = jnp.concatenate([qkv_ref[:, 2 * D_ATTN + g * D_GROUP:2 * D_ATTN + (g + 1) * D_GROUP], pad_rows], axis=0)

        t_idx = lax.broadcasted_iota(jnp.int32, (2 * t_new, hl), 0) & (t_new - 1)
        c_idx = lax.broadcasted_iota(jnp.int32, (2 * t_new, hl), 1)
        mask_c = window_mask(hl + t_idx - c_idx, d)
        mask_n = window_mask(t_new_idx - j_new_idx, d) & (j_new_idx < t_new)
        for hp in range(2):
            cols = slice(hp * LANES, (hp + 1) * LANES)
            q2 = qkv_ref[:, g * D_GROUP + hp * LANES:g * D_GROUP + (hp + 1) * LANES] * SCALE
            o2, lse2 = _pair_attention_cached(q2, kt_ref[cols, :], vt_ref[cols, :], kn[:, cols], vn[:, cols],
                                              mask_c, mask_n)
            ol_refs[2 * g][:, cols] = o2
            ol_refs[2 * g + 1][:, cols] = lse2

        for old_ref, new_ref, rows_new in ((kt_ref, new_cache_refs[2 * g], kn), (vt_ref, new_cache_refs[2 * g + 1], vn)):
            placed = pltpu.roll(rows_new.T, LANES - t_new, axis=1)
            shifted = pltpu.roll(old_ref[...], hl - t_new, axis=1)
            if hl > LANES:
                new_ref[:, 0:hl - LANES] = shifted[:, 0:hl - LANES]
            new_ref[:, hl - LANES:hl] = jnp.where(lane >= LANES - t_new, placed, shifted[:, hl - LANES:hl])


def _mixer_sample(layer, u, qkv, state_conv, caches, prev_new_caches, w, b, lng, lnb):
    n_grp = len(ATTN_GROUPS)
    depth, n_seq = caches[0].shape[:2]
    t_new = u.shape[0] // n_seq
    hist_lens = tuple(c.shape[3] for c in caches[0::2])
    n_hist = CONV_WIDTH - 1

    in_specs = [
        pl.BlockSpec((t_new, D_CONV), lambda n: (n, 0)),
        pl.BlockSpec((t_new, 3 * D_ATTN), lambda n: (n, 0)),
        pl.BlockSpec((None, None, n_hist, D_CONV), lambda n: (layer, n, 0, 0)),
        _const_spec((CONV_WIDTH, D_CONV)),
        _const_spec((1, D_CONV)),
        _const_spec((1, D_CONV)),
        _const_spec((1, D_CONV)),
    ]
    for g in range(n_grp):
        for _ in range(2):
            in_specs.append(pl.BlockSpec((None, None, D_GROUP, hist_lens[g]), lambda n: (layer, n, 0, 0)))
    args = [u, qkv, state_conv, w, b, lng, lnb, *caches]
    aliases = {}
    n_fixed_out = 2 + 2 * n_grp
    if prev_new_caches is not None:
        for j, prev in enumerate(prev_new_caches):
            in_specs.append(pl.BlockSpec(memory_space=pl.ANY))
            aliases[len(args)] = n_fixed_out + j
            args.append(prev)

    out_specs = [
        pl.BlockSpec((t_new, D_CONV), lambda n: (n, 0)),
        pl.BlockSpec((None, n_hist, D_CONV), lambda n: (n, 0, 0)),
    ]
    out_shape = [
        jax.ShapeDtypeStruct((n_seq * t_new, D_CONV), F32),
        jax.ShapeDtypeStruct((n_seq, n_hist, D_CONV), F32),
    ]
    for g in range(n_grp):
        for _ in range(2):
            out_specs.append(pl.BlockSpec((t_new, D_GROUP), lambda n: (n, 0)))
            out_shape.append(jax.ShapeDtypeStruct((n_seq * t_new, D_GROUP), F32))
    for g in range(n_grp):
        for _ in range(2):
            out_specs.append(pl.BlockSpec((None, None, D_GROUP, hist_lens[g]), lambda n: (layer, n, 0, 0)))
            out_shape.append(jax.ShapeDtypeStruct((depth, n_seq, D_GROUP, hist_lens[g]), F32))

    scratch = [pltpu.VMEM((n_hist + t_new + 2, D_CONV), F32)]

    def body(*refs):
        if prev_new_caches is not None:
            k = len(prev_new_caches)
            n_in = 7 + 2 * n_grp
            refs = refs[:n_in] + refs[n_in + k:]
        _mixer_sample_body(t_new, hist_lens, *refs)

    outs = pl.pallas_call(
        body,
        grid=(n_seq,),
        in_specs=in_specs,
        out_specs=out_specs,
        out_shape=out_shape,
        scratch_shapes=scratch,
        input_output_aliases=aliases,
        compiler_params=pltpu.CompilerParams(
            dimension_semantics=("arbitrary",), vmem_limit_bytes=VMEM_LIMIT),
        name=f"mixer_sample_l{layer}",
    )(*args)
    c, state = outs[0], outs[1]
    ol = outs[2:2 + 2 * n_grp]
    new_caches = outs[2 + 2 * n_grp:]
    return c, state, ol, new_caches


FF_CHUNK = 512


def _post_body(tm, per_seq, final, *refs):
    n_grp = len(ATTN_GROUPS)
    x_ref, c_ref = refs[0:2]
    ol_refs = refs[2:2 + 2 * n_grp]
    pos = 2 + 2 * n_grp
    sg_ref = refs[pos]
    (wco_ref, wao_ref, wo_ref, g2_ref, wup_ref, fw_ref, fb_ref, wdn_ref) = refs[pos + 1:pos + 9]
    pos += 9
    if final:
        gf_ref = refs[pos]
        pos += 1
    if per_seq:
        h0_ref, h1_ref = refs[pos:pos + 2]
        pos += 2
    y_ref, up_ref = refs[pos:pos + 2]
    carry = None if per_seq else refs[pos + 2]

    lses = [ol_refs[2 * g + 1][...] for g in range(n_grp)]
    lmax = functools.reduce(jnp.maximum, lses)
    es = [jnp.exp(l - lmax) for l in lses]
    den = functools.reduce(lambda a, b: a + b, es)
    attn = functools.reduce(lambda a, b: a + b, [(es[g] / den) * ol_refs[2 * g][...] for g in range(n_grp)])

    branch_conv = jnp.dot(c_ref[...].astype(BF16), wco_ref[...], preferred_element_type=F32)
    branch_attn = jnp.dot(attn.astype(BF16), wao_ref[...], preferred_element_type=F32)
    mixed = sg_ref[:, 0:D_MODEL] * branch_conv + sg_ref[:, D_MODEL:2 * D_MODEL] * branch_attn
    x1 = x_ref[...] + jnp.dot(mixed.astype(BF16), wo_ref[...], preferred_element_type=F32)

    h = _rms(x1, g2_ref[...]).astype(BF16)
    if not per_seq:
        @pl.when(pl.program_id(0) == 0)
        def _():
            carry[...] = jnp.zeros(carry.shape, F32)
    out_rows = up_ref.shape[0]
    t_in_seq = lax.broadcasted_iota(jnp.int32, (tm, FF_CHUNK), 0) & (SUBLANES - 1)

    def conv3(c0):
        cols = slice(c0, c0 + FF_CHUNK)
        up = jnp.dot(h, wup_ref[:, cols], preferred_element_type=F32)
        up_ref[:, cols] = up[tm - out_rows:, :]
        head = up[0:SUBLANES, :] if per_seq else carry[:, cols]
        ext = jnp.concatenate([head, up], axis=0).reshape(tm // SUBLANES + 1, SUBLANES, FF_CHUNK)
        p1 = _shifted_rows(ext, SUBLANES - 1).reshape(tm, FF_CHUNK)
        p2 = _shifted_rows(ext, SUBLANES - 2).reshape(tm, FF_CHUNK)
        if per_seq:
            p1 = jnp.where(t_in_seq == 0, h1_ref[:, cols], p1)
            p2 = jnp.where(t_in_seq == 0, h0_ref[:, cols], jnp.where(t_in_seq == 1, h1_ref[:, cols], p2))
        else:
            carry[:, cols] = up[tm - SUBLANES:, :]
        return fw_ref[0:1, cols] * p2 + fw_ref[1:2, cols] * p1 + fw_ref[2:3, cols] * up + fb_ref[:, cols]

    acc = jnp.zeros((tm, D_MODEL), F32)
    for c0 in range(0, D_FF, FF_CHUNK):
        val = conv3(c0)
        gate = conv3(D_FF + c0)
        act = (jax.nn.silu(gate) * val).astype(BF16)
        acc = acc + jnp.dot(act, wdn_ref[c0:c0 + FF_CHUNK, :], preferred_element_type=F32)
    x2 = x1 + acc
    if final:
        x2 = _rms(x2, gf_ref[...])
    y_ref[...] = x2


def _post(x, c, ol, sg, wco, wao, wo, g2, wup, fw, fb, wdn, gf, hist, tm):
    rows = x.shape[0]
    n_grp = len(ATTN_GROUPS)
    per_seq = hist is not None
    final = gf is not None
    row_spec = lambda w: pl.BlockSpec((tm, w), lambda i: (i, 0))
    in_specs = [row_spec(D_MODEL), row_spec(D_CONV)] + [row_spec(D_GROUP)] * (2 * n_grp) + [row_spec(2 * D_MODEL)]
    args = [x, c, *ol, sg]
    for a in (wco, wao, wo, g2, wup, fw, fb, wdn):
        in_specs.append(_const_spec(a.shape))
        args.append(a)
    if final:
        in_specs.append(_const_spec(gf.shape))
        args.append(gf)
    if per_seq:
        in_specs += [row_spec(2 * D_FF)] * 2
        args += list(hist)
    up_rows = rows if per_seq else SUBLANES
    up_spec = row_spec(2 * D_FF) if per_seq else pl.BlockSpec((SUBLANES, 2 * D_FF), lambda i: (0, 0))
    scratch = [] if per_seq else [pltpu.VMEM((SUBLANES, 2 * D_FF), F32)]
    return pl.pallas_call(
        functools.partial(_post_body, tm, per_seq, final),
        grid=(rows // tm,),
        in_specs=in_specs,
        out_specs=[row_spec(D_MODEL), up_spec],
        out_shape=[
            jax.ShapeDtypeStruct((rows, D_MODEL), F32),
            jax.ShapeDtypeStruct((up_rows, 2 * D_FF), F32),
        ],
        scratch_shapes=scratch,
        compiler_params=pltpu.CompilerParams(
            dimension_semantics=("arbitrary",), vmem_limit_bytes=VMEM_LIMIT),
        name="post_sample" if per_seq else "post_prompt",
    )(*args)


def kernel(x_prompt, x_sample, state_conv, cache_k_w128, cache_v_w128, cache_k_w512, cache_v_w512, cache_k_w2048, cache_v_w2048, state_ffn_conv, norm_attn_g, w_in, conv_dw_w, conv_dw_b, conv_ln_g, conv_ln_b, w_conv_out, w_attn_out, w_out, norm_ffn_g, w_up, ffn_dw_w, ffn_dw_b, w_down, norm_final_g):
    depth = w_in.shape[0]
    n_b, seq, _ = x_prompt.shape
    n_s, t_new, _ = x_sample.shape
    assert n_b == 1 and seq % ATTN_TILE == 0 and t_new == SUBLANES
    n_grp = len(ATTN_GROUPS)
    caches = [jnp.transpose(c, (0, 1, 3, 4, 2)).reshape(c.shape[:2] + (D_GROUP, c.shape[2])) for c in
              (cache_k_w128, cache_v_w128, cache_k_w512, cache_v_w512, cache_k_w2048, cache_v_w2048)]

    xp = x_prompt.reshape(seq, D_MODEL)
    xs = x_sample.reshape(n_s * t_new, D_MODEL)
    row = lambda a: a.reshape(1, -1)
    gf = row(norm_final_g)

    conv_p, conv_s, ffn_p, ffn_s = [], [], [], []
    kv_p = [[[], []] for _ in range(n_grp)]
    new_caches = None
    for l in range(depth):
        last = l == depth - 1
        w_in_l = w_in[l].astype(BF16)
        wco, wao, wo = w_conv_out[l].astype(BF16), w_attn_out[l].astype(BF16), w_out[l].astype(BF16)
        wup, wdn = w_up[l].astype(BF16), w_down[l].astype(BF16)
        g1, g2 = row(norm_attn_g[l]), row(norm_ffn_g[l])
        cw, cb, lng, lnb = conv_dw_w[l], row(conv_dw_b[l]), row(conv_ln_g[l]), row(conv_ln_b[l])
        fw, fb = ffn_dw_w[l], row(ffn_dw_b[l])
        post_w = (wco, wao, wo, g2, wup, fw, fb, wdn, gf if last else None)

        cw8 = jnp.broadcast_to(cw[:, None, :], (CONV_WIDTH, SUBLANES, D_CONV))
        u, qkv, sg = _inproj(xp, g1, w_in_l, 512)
        c = _conv_prompt(u, cw8, cb, lng, lnb, 512)
        ol = []
        for g, (_, d) in enumerate(ATTN_GROUPS):
            ol += _attn_prompt(qkv, g, d)
        xp, up_tail = _post(xp, c, ol, sg, *post_w, None, 512)
        conv_p.append(u[seq - (CONV_WIDTH - 1):].reshape(1, CONV_WIDTH - 1, D_CONV))
        ffn_p.append(up_tail[SUBLANES - (FFN_CONV_WIDTH - 1):].reshape(1, FFN_CONV_WIDTH - 1, 2 * D_FF))
        for g, (window, _) in enumerate(ATTN_GROUPS):
            keep = min(window, seq)
            for j in range(2):
                c0 = (1 + j) * D_ATTN + g * D_GROUP
                kv_p[g][j].append(qkv[seq - keep:, c0:c0 + D_GROUP].reshape(1, keep, HEADS_PER_GROUP, HEAD_DIM))

        us, qkvs, sgs = _inproj(xs, g1, w_in_l, n_s * t_new)
        cs, state_s, ols, new_caches = _mixer_sample(l, us, qkvs, state_conv, caches, new_caches, cw, cb, lng, lnb)
        hist = tuple(jnp.repeat(state_ffn_conv[l][:, j], t_new, axis=0) for j in range(FFN_CONV_WIDTH - 1))
        xs, up_s = _post(xs, cs, ols, sgs, *post_w, hist, n_s * t_new)
        conv_s.append(state_s)
        ffn_s.append(up_s.reshape(n_s, t_new, 2 * D_FF)[:, t_new - (FFN_CONV_WIDTH - 1):])

    st = lambda lst: jnp.stack(lst, axis=0)
    kv_s = [jnp.transpose(nc.reshape(nc.shape[:2] + (HEADS_PER_GROUP, HEAD_DIM, nc.shape[3])), (0, 1, 4, 2, 3))
            for nc in new_caches]
    return (xp.reshape(n_b, seq, D_MODEL), xs.reshape(n_s, t_new, D_MODEL),
            st(conv_p), st(conv_s),
            st(kv_p[0][0]), st(kv_p[0][1]), kv_s[0], kv_s[1],
            st(kv_p[1][0]), st(kv_p[1][1]), kv_s[2], kv_s[3],
            st(kv_p[2][0]), st(kv_p[2][1]), kv_s[4], kv_s[5],
            st(ffn_p), st(ffn_s))
```

```python
import functools

import jax
import jax.numpy as jnp
from jax import lax
from jax.experimental import pallas as pl
from jax.experimental.pallas import tpu as pltpu

F32 = jnp.float32
BF16 = jnp.bfloat16

D_MODEL = 1024
D_CONV = 1024
CONV_WIDTH = 31
HEAD_DIM = 64
HEADS_PER_GROUP = 4
ATTN_GROUPS = ((128, 1), (512, 4), (2048, 16))
N_KEYS = 129
D_GROUP = HEADS_PER_GROUP * HEAD_DIM
D_ATTN = len(ATTN_GROUPS) * D_GROUP
D_FF = 3 * D_MODEL
FFN_CONV_WIDTH = 3
EPS = 1e-6
SCALE = HEAD_DIM ** -0.5

LANES = 128
SUBLANES = 8
VMEM_LIMIT = 56 * 1024 * 1024

CONV_HALO = 32
ATTN_TILE = 2048
ATTN_BLOCK = 128


def _const_spec(shape):
    nd = len(shape)
    return pl.BlockSpec(shape, lambda *_: (0,) * nd, pipeline_mode=pl.Buffered(1))


def _rms(x, g):
    return x * lax.rsqrt(jnp.mean(x * x, axis=-1, keepdims=True) + EPS) * g


PROJ_CHUNK = 512


def _glu_project(h, w_ref, u_ref, row0):
    rows = h.shape[0]
    for j in range(0, D_CONV, PROJ_CHUNK):
        a = jnp.dot(h, w_ref[:, j:j + PROJ_CHUNK], preferred_element_type=F32)
        b = jnp.dot(h, w_ref[:, D_CONV + j:D_CONV + j + PROJ_CHUNK], preferred_element_type=F32)
        u_ref[row0:row0 + rows, j:j + PROJ_CHUNK] = a * jax.nn.sigmoid(b)


def _qkv_gate_tasks(h, w_ref, qkv_ref, sg_ref):
    tasks = []
    q0 = 2 * D_CONV
    g0 = q0 + 3 * D_ATTN

    def qkv_task(j, width):
        def run():
            qkv_ref[:, j:j + width] = jnp.dot(h, w_ref[:, q0 + j:q0 + j + width], preferred_element_type=F32)
        return run

    def gate_task(j):
        def run():
            sg_ref[:, j:j + PROJ_CHUNK] = jax.nn.sigmoid(
                jnp.dot(h, w_ref[:, g0 + j:g0 + j + PROJ_CHUNK], preferred_element_type=F32))
        return run

    for j in range(0, 3 * D_ATTN, PROJ_CHUNK):
        tasks.append(qkv_task(j, min(PROJ_CHUNK, 3 * D_ATTN - j)))
    for j in range(0, 2 * D_MODEL, PROJ_CHUNK):
        tasks.append(gate_task(j))
    return tasks


def _inproj_body(x_ref, g_ref, w_ref, u_ref, qkv_ref, sg_ref):
    h = _rms(x_ref[...], g_ref[...]).astype(BF16)
    _glu_project(h, w_ref, u_ref, 0)
    for task in _qkv_gate_tasks(h, w_ref, qkv_ref, sg_ref):
        task()


def _inproj(x, g, w_in, tm):
    rows = x.shape[0]
    d_in = w_in.shape[1]
    return pl.pallas_call(
        _inproj_body,
        grid=(rows // tm,),
        in_specs=[
            pl.BlockSpec((tm, D_MODEL), lambda i: (i, 0)),
            _const_spec((1, D_MODEL)),
            _const_spec((D_MODEL, d_in)),
        ],
        out_specs=[
            pl.BlockSpec((tm, D_CONV), lambda i: (i, 0)),
            pl.BlockSpec((tm, 3 * D_ATTN), lambda i: (i, 0)),
            pl.BlockSpec((tm, 2 * D_MODEL), lambda i: (i, 0)),
        ],
        out_shape=[
            jax.ShapeDtypeStruct((rows, D_CONV), F32),
            jax.ShapeDtypeStruct((rows, 3 * D_ATTN), F32),
            jax.ShapeDtypeStruct((rows, 2 * D_MODEL), F32),
        ],
        compiler_params=pltpu.CompilerParams(
            dimension_semantics=("parallel",), vmem_limit_bytes=VMEM_LIMIT),
        name="inproj",
    )(x, g, w_in)


def _ln_swish(y, g, b):
    mu = jnp.mean(y, axis=-1, keepdims=True)
    yc = y - mu
    var = jnp.mean(yc * yc, axis=-1, keepdims=True)
    z = yc * lax.rsqrt(var + EPS) * g + b
    return z * jax.nn.sigmoid(z)


def _shifted_rows(blk, s):
    if s == 0:
        return blk[:-1]
    rot = pltpu.roll(blk, SUBLANES - s, axis=1)
    sub = lax.broadcasted_iota(jnp.int32, rot[:-1].shape, 1)
    return jnp.where(sub < SUBLANES - s, rot[:-1], rot[1:])


def _conv31_lanes(buf, w8_ref, b_ref, ybuf, rows, l0):
    first = CONV_HALO - (CONV_WIDTH - 1)
    rc = 256
    nv = rc // SUBLANES
    lanes = slice(l0, l0 + LANES)

    def row_chunk(i, carry):
        r0 = pl.multiple_of(i * rc, rc)
        blk = buf[pl.ds(r0, rc + CONV_HALO + SUBLANES), lanes].reshape(nv + 5, SUBLANES, LANES)
        acc = jnp.broadcast_to(b_ref[:, lanes], (nv, SUBLANES, LANES))
        for s in range(SUBLANES):
            sh = _shifted_rows(blk, s)
            for a in range(5):
                k = SUBLANES * a + s - first
                if 0 <= k < CONV_WIDTH:
                    acc = acc + w8_ref[k, :, lanes] * sh[a:a + nv]
        ybuf[pl.ds(r0, rc), lanes] = acc.reshape(rc, LANES)
        return carry

    lax.fori_loop(0, rows // rc, row_chunk, 0)


def _conv_prompt_body(tm, u_ref, cw8_ref, cb_ref, lng_ref, lnb_ref, c_ref, ubuf, ybuf):
    @pl.when(pl.program_id(0) == 0)
    def _():
        ubuf[0:CONV_HALO, :] = jnp.zeros((CONV_HALO, D_CONV), F32)
        ubuf[CONV_HALO + tm:, :] = jnp.zeros((SUBLANES, D_CONV), F32)

    ubuf[CONV_HALO:CONV_HALO + tm, :] = u_ref[...]
    for l0 in range(0, D_CONV, LANES):
        _conv31_lanes(ubuf, cw8_ref, cb_ref, ybuf, tm, l0)
    c_ref[...] = _ln_swish(ybuf[...], lng_ref[...], lnb_ref[...]).astype(c_ref.dtype)
    ubuf[0:CONV_HALO, :] = ubuf[tm:tm + CONV_HALO, :]


def _conv_prompt(u, cw8, cb, lng, lnb, tm):
    rows = u.shape[0]
    return pl.pallas_call(
        functools.partial(_conv_prompt_body, tm),
        grid=(rows // tm,),
        in_specs=[
            pl.BlockSpec((tm, D_CONV), lambda i: (i, 0)),
            _const_spec((CONV_WIDTH, SUBLANES, D_CONV)),
            _const_spec((1, D_CONV)),
            _const_spec((1, D_CONV)),
            _const_spec((1, D_CONV)),
        ],
        out_specs=pl.BlockSpec((tm, D_CONV), lambda i: (i, 0)),
        out_shape=jax.ShapeDtypeStruct((rows, D_CONV), BF16),
        scratch_shapes=[
            pltpu.VMEM((CONV_HALO + tm + SUBLANES, D_CONV), F32),
            pltpu.VMEM((tm, D_CONV), F32),
        ],
        compiler_params=pltpu.CompilerParams(
            dimension_semantics=("arbitrary",), vmem_limit_bytes=VMEM_LIMIT),
        name="conv_prompt",
    )(u, cw8, cb, lng, lnb)


def _pair_attention(q2, k2, v2, mask):
    m_rows = q2.shape[0]
    lane = lax.broadcasted_iota(jnp.int32, q2.shape, 1)
    zero = jnp.zeros_like(q2)
    qq = jnp.concatenate([jnp.where(lane < HEAD_DIM, q2, zero), jnp.where(lane >= HEAD_DIM, q2, zero)], axis=0)
    s = lax.dot_general(qq.astype(BF16), k2.astype(BF16), (((1,), (1,)), ((), ())), preferred_element_type=F32)
    s = jnp.where(mask, s, -jnp.inf)
    mx = jnp.max(s, axis=-1, keepdims=True)
    p = jnp.exp(s - mx)
    l = jnp.sum(p, axis=-1, keepdims=True)
    pv = jnp.dot(p.astype(BF16), v2.astype(BF16), preferred_element_type=F32)
    o = pv / l
    lse = jnp.broadcast_to(mx + jnp.log(l), o.shape)
    first = lane < HEAD_DIM
    return (jnp.where(first, o[:m_rows], o[m_rows:]),
            jnp.where(first, lse[:m_rows], lse[m_rows:]))


def _pair_attention_cached(q2, kt, vt, kn, vn, mask_c, mask_n):
    m_rows = q2.shape[0]
    lane = lax.broadcasted_iota(jnp.int32, q2.shape, 1)
    zero = jnp.zeros_like(q2)
    qq = jnp.concatenate([jnp.where(lane < HEAD_DIM, q2, zero), jnp.where(lane >= HEAD_DIM, q2, zero)],
                         axis=0).astype(BF16)
    nt = (((1,), (1,)), ((), ()))
    s_c = jnp.where(mask_c, jnp.dot(qq, kt.astype(BF16), preferred_element_type=F32), -jnp.inf)
    s_n = jnp.where(mask_n, lax.dot_general(qq, kn.astype(BF16), nt, preferred_element_type=F32), -jnp.inf)
    mx = jnp.maximum(jnp.max(s_c, axis=-1, keepdims=True), jnp.max(s_n, axis=-1, keepdims=True))
    p_c = jnp.exp(s_c - mx)
    p_n = jnp.exp(s_n - mx)
    l = jnp.sum(p_c, axis=-1, keepdims=True) + jnp.sum(p_n, axis=-1, keepdims=True)
    pv = (lax.dot_general(p_c.astype(BF16), vt.astype(BF16), nt, preferred_element_type=F32)
          + jnp.dot(p_n.astype(BF16), vn.astype(BF16), preferred_element_type=F32))
    o = pv / l
    lse = jnp.broadcast_to(mx + jnp.log(l), o.shape)
    first = lane < HEAD_DIM
    return (jnp.where(first, o[:m_rows], o[m_rows:]),
            jnp.where(first, lse[:m_rows], lse[m_rows:]))


def _attn_prompt_body(d, q_ref, k_ref, v_ref, o_ref, lse_ref, qs, kbuf, vbuf, os_, ls_):
    tq = ATTN_TILE
    blk = ATTN_BLOCK
    span = blk * d
    i = pl.program_id(0)

    @pl.when(i == 0)
    def _():
        kbuf[:, 0:tq, :] = jnp.zeros((2, tq, LANES), F32)
        vbuf[:, 0:tq, :] = jnp.zeros((2, tq, LANES), F32)

    for hp in range(2):
        cols = slice(hp * LANES, (hp + 1) * LANES)
        qs[hp] = q_ref[:, cols] * SCALE
        kbuf[hp, tq:2 * tq, :] = k_ref[:, cols]
        vbuf[hp, tq:2 * tq, :] = v_ref[:, cols]

    row = lax.broadcasted_iota(jnp.int32, (2 * blk, 2 * blk), 0) & (blk - 1)
    col = lax.broadcasted_iota(jnp.int32, (2 * blk, 2 * blk), 1)
    band = (col >= row) & (col <= row + (N_KEYS - 1))

    def problem(j, carry):
        b = j // d
        r = j % d
        qstart = b * span + r
        kstart = tq + (b - 1) * span + r
        cmin = jnp.where((i == 0) & (b == 0), blk, 0)
        mask = band & (col >= cmin)
        for hp in range(2):
            if d == 1:
                qidx = pl.ds(pl.multiple_of(qstart, blk), blk)
                kidx = pl.ds(pl.multiple_of(kstart, blk), 2 * blk)
            else:
                qidx = pl.ds(qstart, blk, stride=d)
                kidx = pl.ds(kstart, 2 * blk, stride=d)
            o2, lse2 = _pair_attention(qs[hp, qidx, :], kbuf[hp, kidx, :], vbuf[hp, kidx, :], mask)
            os_[hp, qidx, :] = o2
            ls_[hp, qidx, :] = lse2
        return carry

    lax.fori_loop(0, tq // blk, problem, 0, unroll=4)

    for hp in range(2):
        cols = slice(hp * LANES, (hp + 1) * LANES)
        o_ref[:, cols] = os_[hp]
        lse_ref[:, cols] = ls_[hp]
    kbuf[:, 0:tq, :] = kbuf[:, tq:2 * tq, :]
    vbuf[:, 0:tq, :] = vbuf[:, tq:2 * tq, :]


def _attn_prompt(qkv, g, d):
    rows = qkv.shape[0]
    tq = ATTN_TILE
    n_grp = len(ATTN_GROUPS)
    slab = pltpu.VMEM((2, tq, LANES), F32)
    slab2 = pltpu.VMEM((2, 2 * tq, LANES), F32)
    return pl.pallas_call(
        functools.partial(_attn_prompt_body, d),
        grid=(rows // tq,),
        in_specs=[
            pl.BlockSpec((tq, D_GROUP), lambda i: (i, g)),
            pl.BlockSpec((tq, D_GROUP), lambda i: (i, n_grp + g)),
            pl.BlockSpec((tq, D_GROUP), lambda i: (i, 2 * n_grp + g)),
        ],
        out_specs=[
            pl.BlockSpec((tq, D_GROUP), lambda i: (i, 0)),
            pl.BlockSpec((tq, D_GROUP), lambda i: (i, 0)),
        ],
        out_shape=[
            jax.ShapeDtypeStruct((rows, D_GROUP), F32),
            jax.ShapeDtypeStruct((rows, D_GROUP), F32),
        ],
        scratch_shapes=[slab, slab2, slab2, slab, slab],
        compiler_params=pltpu.CompilerParams(
            dimension_semantics=("arbitrary",), vmem_limit_bytes=VMEM_LIMIT),
        name=f"attn_prompt_d{d}",
    )(qkv, qkv, qkv)


def _mixer_sample_body(t_new, hist_lens, *refs):
    n_grp = len(ATTN_GROUPS)
    (u_ref, qkv_ref, hist_ref, w_ref, b_ref, lng_ref, lnb_ref) = refs[:7]
    cache_refs = refs[7:7 + 2 * n_grp]
    pos = 7 + 2 * n_grp
    c_ref, state_ref = refs[pos:pos + 2]
    ol_refs = refs[pos + 2:pos + 2 + 2 * n_grp]
    new_cache_refs = refs[pos + 2 + 2 * n_grp:pos + 2 + 4 * n_grp]
    xs = refs[pos + 2 + 4 * n_grp]

    n_hist = CONV_WIDTH - 1
    xs[0:n_hist, :] = hist_ref[...]
    xs[n_hist:n_hist + t_new, :] = u_ref[...]
    acc = jnp.broadcast_to(b_ref[...], (t_new, D_CONV))
    for k in range(CONV_WIDTH):
        acc = acc + w_ref[k:k + 1, :] * xs[k:k + t_new, :]
    c_ref[...] = _ln_swish(acc, lng_ref[...], lnb_ref[...])
    state_ref[...] = xs[t_new:t_new + n_hist, :]

    def window_mask(delta, d):
        return (delta >= 0) & ((delta & (d - 1)) == 0) & (delta <= (N_KEYS - 1) * d)

    pad_rows = jnp.zeros((LANES - t_new, D_GROUP), F32)
    t_new_idx = lax.broadcasted_iota(jnp.int32, (2 * t_new, LANES), 0) & (t_new - 1)
    j_new_idx = lax.broadcasted_iota(jnp.int32, (2 * t_new, LANES), 1)
    lane = lax.broadcasted_iota(jnp.int32, (D_GROUP, LANES), 1)
    for g, (_, d) in enumerate(ATTN_GROUPS):
        hl = hist_lens[g]
        kt_ref, vt_ref = cache_refs[2 * g], cache_refs[2 * g + 1]
        kn = jnp.concatenate([qkv_ref[:, D_ATTN + g * D_GROUP:D_ATTN + (g + 1) * D_GROUP], pad_rows], axis=0)
        vn = jnp.concatenate([qkv_ref[:, 2 * D_ATTN + g * D_GROUP:2 * D_ATTN + (g + 1) * D_GROUP], pad_rows], axis=0)

        t_idx = lax.broadcasted_iota(jnp.int32, (2 * t_new, hl), 0) & (t_new - 1)
        c_idx = lax.broadcasted_iota(jnp.int32, (2 * t_new, hl), 1)
        mask_c = window_mask(hl + t_idx - c_idx, d)
        mask_n = window_mask(t_new_idx - j_new_idx, d) & (j_new_idx < t_new)
        for hp in range(2):
            cols = slice(hp * LANES, (hp + 1) * LANES)
            q2 = qkv_ref[:, g * D_GROUP + hp * LANES:g * D_GROUP + (hp + 1) * LANES] * SCALE
            o2, lse2 = _pair_attention_cached(q2, kt_ref[cols, :], vt_ref[cols, :], kn[:, cols], vn[:, cols],
                                              mask_c, mask_n)
            ol_refs[2 * g][:, cols] = o2
            ol_refs[2 * g + 1][:, cols] = lse2

        for old_ref, new_ref, rows_new in ((kt_ref, new_cache_refs[2 * g], kn), (vt_ref, new_cache_refs[2 * g + 1], vn)):
            placed = pltpu.roll(rows_new.T, LANES - t_new, axis=1)
            shifted = pltpu.roll(old_ref[...], hl - t_new, axis=1)
            if hl > LANES:
                new_ref[:, 0:hl - LANES] = shifted[:, 0:hl - LANES]
            new_ref[:, hl - LANES:hl] = jnp.where(lane >= LANES - t_new, placed, shifted[:, hl - LANES:hl])


def _mixer_sample(layer, u, qkv, state_conv, caches, prev_new_caches, w, b, lng, lnb):
    n_grp = len(ATTN_GROUPS)
    depth, n_seq = caches[0].shape[:2]
    t_new = u.shape[0] // n_seq
    hist_lens = tuple(c.shape[3] for c in caches[0::2])
    n_hist = CONV_WIDTH - 1

    in_specs = [
        pl.BlockSpec((t_new, D_CONV), lambda n: (n, 0)),
        pl.BlockSpec((t_new, 3 * D_ATTN), lambda n: (n, 0)),
        pl.BlockSpec((None, None, n_hist, D_CONV), lambda n: (layer, n, 0, 0)),
        _const_spec((CONV_WIDTH, D_CONV)),
        _const_spec((1, D_CONV)),
        _const_spec((1, D_CONV)),
        _const_spec((1, D_CONV)),
    ]
    for g in range(n_grp):
        for _ in range(2):
            in_specs.append(pl.BlockSpec((None, None, D_GROUP, hist_lens[g]), lambda n: (layer, n, 0, 0)))
    args = [u, qkv, state_conv, w, b, lng, lnb, *caches]
    aliases = {}
    n_fixed_out = 2 + 2 * n_grp
    if prev_new_caches is not None:
        for j, prev in enumerate(prev_new_caches):
            in_specs.append(pl.BlockSpec(memory_space=pl.ANY))
            aliases[len(args)] = n_fixed_out + j
            args.append(prev)

    out_specs = [
        pl.BlockSpec((t_new, D_CONV), lambda n: (n, 0)),
        pl.BlockSpec((None, n_hist, D_CONV), lambda n: (n, 0, 0)),
    ]
    out_shape = [
        jax.ShapeDtypeStruct((n_seq * t_new, D_CONV), F32),
        jax.ShapeDtypeStruct((n_seq, n_hist, D_CONV), F32),
    ]
    for g in range(n_grp):
        for _ in range(2):
            out_specs.append(pl.BlockSpec((t_new, D_GROUP), lambda n: (n, 0)))
            out_shape.append(jax.ShapeDtypeStruct((n_seq * t_new, D_GROUP), F32))
    for g in range(n_grp):
        for _ in range(2):
            out_specs.append(pl.BlockSpec((None, None, D_GROUP, hist_lens[g]), lambda n: (layer, n, 0, 0)))
            out_shape.append(jax.ShapeDtypeStruct((depth, n_seq, D_GROUP, hist_lens[g]), F32))

    scratch = [pltpu.VMEM((n_hist + t_new + 2, D_CONV), F32)]

    def body(*refs):
        if prev_new_caches is not None:
            k = len(prev_new_caches)
            n_in = 7 + 2 * n_grp
            refs = refs[:n_in] + refs[n_in + k:]
        _mixer_sample_body(t_new, hist_lens, *refs)

    outs = pl.pallas_call(
        body,
        grid=(n_seq,),
        in_specs=in_specs,
        out_specs=out_specs,
        out_shape=out_shape,
        scratch_shapes=scratch,
        input_output_aliases=aliases,
        compiler_params=pltpu.CompilerParams(
            dimension_semantics=("arbitrary",), vmem_limit_bytes=VMEM_LIMIT),
        name=f"mixer_sample_l{layer}",
    )(*args)
    c, state = outs[0], outs[1]
    ol = outs[2:2 + 2 * n_grp]
    new_caches = outs[2 + 2 * n_grp:]
    return c, state, ol, new_caches


FF_CHUNK = 512


def _post_body(tm, per_seq, final, *refs):
    n_grp = len(ATTN_GROUPS)
    x_ref, c_ref = refs[0:2]
    ol_refs = refs[2:2 + 2 * n_grp]
    pos = 2 + 2 * n_grp
    sg_ref = refs[pos]
    (wco_ref, wao_ref, wo_ref, g2_ref, wup_ref, fw_ref, fb_ref, wdn_ref) = refs[pos + 1:pos + 9]
    pos += 9
    if final:
        gf_ref = refs[pos]
        pos += 1
    if per_seq:
        h0_ref, h1_ref = refs[pos:pos + 2]
        pos += 2
    y_ref, up_ref = refs[pos:pos + 2]
    carry = None if per_seq else refs[pos + 2]

    lses = [ol_refs[2 * g + 1][...] for g in range(n_grp)]
    lmax = functools.reduce(jnp.maximum, lses)
    es = [jnp.exp(l - lmax) for l in lses]
    den = functools.reduce(lambda a, b: a + b, es)
    attn = functools.reduce(lambda a, b: a + b, [(es[g] / den) * ol_refs[2 * g][...] for g in range(n_grp)])

    branch_conv = jnp.dot(c_ref[...].astype(BF16), wco_ref[...], preferred_element_type=F32)
    branch_attn = jnp.dot(attn.astype(BF16), wao_ref[...], preferred_element_type=F32)
    mixed = sg_ref[:, 0:D_MODEL] * branch_conv + sg_ref[:, D_MODEL:2 * D_MODEL] * branch_attn
    x1 = x_ref[...] + jnp.dot(mixed.astype(BF16), wo_ref[...], preferred_element_type=F32)

    h = _rms(x1, g2_ref[...]).astype(BF16)
    if not per_seq:
        @pl.when(pl.program_id(0) == 0)
        def _():
            carry[...] = jnp.zeros(carry.shape, F32)
    out_rows = up_ref.shape[0]
    t_in_seq = lax.broadcasted_iota(jnp.int32, (tm, FF_CHUNK), 0) & (SUBLANES - 1)

    def conv3(c0):
        cols = slice(c0, c0 + FF_CHUNK)
        up = jnp.dot(h, wup_ref[:, cols], preferred_element_type=F32)
        up_ref[:, cols] = up[tm - out_rows:, :]
        head = up[0:SUBLANES, :] if per_seq else carry[:, cols]
        ext = jnp.concatenate([head, up], axis=0).reshape(tm // SUBLANES + 1, SUBLANES, FF_CHUNK)
        p1 = _shifted_rows(ext, SUBLANES - 1).reshape(tm, FF_CHUNK)
        p2 = _shifted_rows(ext, SUBLANES - 2).reshape(tm, FF_CHUNK)
        if per_seq:
            p1 = jnp.where(t_in_seq == 0, h1_ref[:, cols], p1)
            p2 = jnp.where(t_in_seq == 0, h0_ref[:, cols], jnp.where(t_in_seq == 1, h1_ref[:, cols], p2))
        else:
            carry[:, cols] = up[tm - SUBLANES:, :]
        return fw_ref[0:1, cols] * p2 + fw_ref[1:2, cols] * p1 + fw_ref[2:3, cols] * up + fb_ref[:, cols]

    acc = jnp.zeros((tm, D_MODEL), F32)
    for c0 in range(0, D_FF, FF_CHUNK):
        val = conv3(c0)
        gate = conv3(D_FF + c0)
        act = (jax.nn.silu(gate) * val).astype(BF16)
        acc = acc + jnp.dot(act, wdn_ref[c0:c0 + FF_CHUNK, :], preferred_element_type=F32)
    x2 = x1 + acc
    if final:
        x2 = _rms(x2, gf_ref[...])
    y_ref[...] = x2


def _post(x, c, ol, sg, wco, wao, wo, g2, wup, fw, fb, wdn, gf, hist, tm):
    rows = x.shape[0]
    n_grp = len(ATTN_GROUPS)
    per_seq = hist is not None
    final = gf is not None
    row_spec = lambda w: pl.BlockSpec((tm, w), lambda i: (i, 0))
    in_specs = [row_spec(D_MODEL), row_spec(D_CONV)] + [row_spec(D_GROUP)] * (2 * n_grp) + [row_spec(2 * D_MODEL)]
    args = [x, c, *ol, sg]
    for a in (wco, wao, wo, g2, wup, fw, fb, wdn):
        in_specs.append(_const_spec(a.shape))
        args.append(a)
    if final:
        in_specs.append(_const_spec(gf.shape))
        args.append(gf)
    if per_seq:
        in_specs += [row_spec(2 * D_FF)] * 2
        args += list(hist)
    up_rows = rows if per_seq else SUBLANES
    up_spec = row_spec(2 * D_FF) if per_seq else pl.BlockSpec((SUBLANES, 2 * D_FF), lambda i: (0, 0))
    scratch = [] if per_seq else [pltpu.VMEM((SUBLANES, 2 * D_FF), F32)]
    return pl.pallas_call(
        functools.partial(_post_body, tm, per_seq, final),
        grid=(rows // tm,),
        in_specs=in_specs,
        out_specs=[row_spec(D_MODEL), up_spec],
        out_shape=[
            jax.ShapeDtypeStruct((rows, D_MODEL), F32),
            jax.ShapeDtypeStruct((up_rows, 2 * D_FF), F32),
        ],
        scratch_shapes=scratch,
        compiler_params=pltpu.CompilerParams(
            dimension_semantics=("arbitrary",), vmem_limit_bytes=VMEM_LIMIT),
        name="post_sample" if per_seq else "post_prompt",
    )(*args)


def kernel(x_prompt, x_sample, state_conv, cache_k_w128, cache_v_w128, cache_k_w512, cache_v_w512, cache_k_w2048, cache_v_w2048, state_ffn_conv, norm_attn_g, w_in, conv_dw_w, conv_dw_b, conv_ln_g, conv_ln_b, w_conv_out, w_attn_out, w_out, norm_ffn_g, w_up, ffn_dw_w, ffn_dw_b, w_down, norm_final_g):
    depth = w_in.shape[0]
    n_b, seq, _ = x_prompt.shape
    n_s, t_new, _ = x_sample.shape
    assert n_b == 1 and seq % ATTN_TILE == 0 and t_new == SUBLANES
    n_grp = len(ATTN_GROUPS)
    caches = [jnp.transpose(c, (0, 1, 3, 4, 2)).reshape(c.shape[:2] + (D_GROUP, c.shape[2])) for c in
              (cache_k_w128, cache_v_w128, cache_k_w512, cache_v_w512, cache_k_w2048, cache_v_w2048)]

    xp = x_prompt.reshape(seq, D_MODEL)
    xs = x_sample.reshape(n_s * t_new, D_MODEL)
    row = lambda a: a.reshape(1, -1)
    gf = row(norm_final_g)

    conv_p, conv_s, ffn_p, ffn_s = [], [], [], []
    kv_p = [[[], []] for _ in range(n_grp)]
    new_caches = None
    for l in range(depth):
        last = l == depth - 1
        w_in_l = w_in[l].astype(BF16)
        wco, wao, wo = w_conv_out[l].astype(BF16), w_attn_out[l].astype(BF16), w_out[l].astype(BF16)
        wup, wdn = w_up[l].astype(BF16), w_down[l].astype(BF16)
        g1, g2 = row(norm_attn_g[l]), row(norm_ffn_g[l])
        cw, cb, lng, lnb = conv_dw_w[l], row(conv_dw_b[l]), row(conv_ln_g[l]), row(conv_ln_b[l])
        fw, fb = ffn_dw_w[l], row(ffn_dw_b[l])
        post_w = (wco, wao, wo, g2, wup, fw, fb, wdn, gf if last else None)

        cw8 = jnp.broadcast_to(cw[:, None, :], (CONV_WIDTH, SUBLANES, D_CONV))
        u, qkv, sg = _inproj(xp, g1, w_in_l, 512)
        c = _conv_prompt(u, cw8, cb, lng, lnb, 512)
        ol = []
        for g, (_, d) in enumerate(ATTN_GROUPS):
            ol += _attn_prompt(qkv, g, d)
        xp, up_tail = _post(xp, c, ol, sg, *post_w, None, 512)
        conv_p.append(u[seq - (CONV_WIDTH - 1):].reshape(1, CONV_WIDTH - 1, D_CONV))
        ffn_p.append(up_tail[SUBLANES - (FFN_CONV_WIDTH - 1):].reshape(1, FFN_CONV_WIDTH - 1, 2 * D_FF))
        for g, (window, _) in enumerate(ATTN_GROUPS):
            keep = min(window, seq)
            for j in range(2):
                c0 = (1 + j) * D_ATTN + g * D_GROUP
                kv_p[g][j].append(qkv[seq - keep:, c0:c0 + D_GROUP].reshape(1, keep, HEADS_PER_GROUP, HEAD_DIM))

        us, qkvs, sgs = _inproj(xs, g1, w_in_l, n_s * t_new)
        cs, state_s, ols, new_caches = _mixer_sample(l, us, qkvs, state_conv, caches, new_caches, cw, cb, lng, lnb)
        hist = tuple(jnp.repeat(state_ffn_conv[l][:, j], t_new, axis=0) for j in range(FFN_CONV_WIDTH - 1))
        xs, up_s = _post(xs, cs, ols, sgs, *post_w, hist, n_s * t_new)
        conv_s.append(state_s)
        ffn_s.append(up_s.reshape(n_s, t_new, 2 * D_FF)[:, t_new - (FFN_CONV_WIDTH - 1):])

    st = lambda lst: jnp.stack(lst, axis=0)
    kv_s = [jnp.transpose(nc.reshape(nc.shape[:2] + (HEADS_PER_GROUP, HEAD_DIM, nc.shape[3])), (0, 1, 4, 2, 3))
            for nc in new_caches]
    return (xp.reshape(n_b, seq, D_MODEL), xs.reshape(n_s, t_new, D_MODEL),
            st(conv_p), st(conv_s),
            st(kv_p[0][0]), st(kv_p[0][1]), kv_s[0], kv_s[1],
            st(kv_p[1][0]), st(kv_p[1][1]), kv_s[2], kv_s[3],
            st(kv_p[2][0]), st(kv_p[2][1]), kv_s[4], kv_s[5],
            st(ffn_p), st(ffn_s))
```

```python
import functools

import jax
import jax.numpy as jnp
from jax import lax
from jax.experimental import pallas as pl
from jax.experimental.pallas import tpu as pltpu

F32 = jnp.float32
BF16 = jnp.bfloat16

D_MODEL = 1024
D_CONV = 1024
CONV_WIDTH = 31
HEAD_DIM = 64
HEADS_PER_GROUP = 4
ATTN_GROUPS = ((128, 1), (512, 4), (2048, 16))
N_KEYS = 129
D_GROUP = HEADS_PER_GROUP * HEAD_DIM
D_ATTN = len(ATTN_GROUPS) * D_GROUP
D_FF = 3 * D_MODEL
FFN_CONV_WIDTH = 3
EPS = 1e-6
SCALE = HEAD_DIM ** -0.5

LANES = 128
SUBLANES = 8
VMEM_LIMIT = 56 * 1024 * 1024

ROW_TILE = 512
CONV_HALO = 32
CONV_ROW_CHUNK = 256
ATTN_TILE = 2048
ATTN_BLOCK = 128


def _const_spec(shape):
    nd = len(shape)
    return pl.BlockSpec(shape, lambda *_: (0,) * nd, pipeline_mode=pl.Buffered(1))


def _layer_spec(stacked, layer):
    rest = stacked.shape[1:]
    return pl.BlockSpec((None,) + rest, lambda *_: (layer,) + (0,) * len(rest), pipeline_mode=pl.Buffered(1))


def _rms(x, g):
    return x * lax.rsqrt(jnp.mean(x * x, axis=-1, keepdims=True) + EPS) * g


PROJ_CHUNK = 512


def _glu_project(h, w_ref, u_ref, row0):
    rows = h.shape[0]
    for j in range(0, D_CONV, PROJ_CHUNK):
        a = jnp.dot(h, w_ref[:, j:j + PROJ_CHUNK], preferred_element_type=F32)
        b = jnp.dot(h, w_ref[:, D_CONV + j:D_CONV + j + PROJ_CHUNK], preferred_element_type=F32)
        u_ref[row0:row0 + rows, j:j + PROJ_CHUNK] = a * jax.nn.sigmoid(b)


def _qkv_gate_tasks(h, w_ref, qkv_ref, sg_ref):
    tasks = []
    q0 = 2 * D_CONV
    g0 = q0 + 3 * D_ATTN

    def qkv_task(j, width):
        def run():
            qkv_ref[:, j:j + width] = jnp.dot(h, w_ref[:, q0 + j:q0 + j + width], preferred_element_type=F32)
        return run

    def gate_task(j):
        def run():
            sg_ref[:, j:j + PROJ_CHUNK] = jax.nn.sigmoid(
                jnp.dot(h, w_ref[:, g0 + j:g0 + j + PROJ_CHUNK], preferred_element_type=F32))
        return run

    for j in range(0, 3 * D_ATTN, PROJ_CHUNK):
        tasks.append(qkv_task(j, min(PROJ_CHUNK, 3 * D_ATTN - j)))
    for j in range(0, 2 * D_MODEL, PROJ_CHUNK):
        tasks.append(gate_task(j))
    return tasks


def _inproj_body(x_ref, g_ref, w_ref, u_ref, qkv_ref, sg_ref):
    h = _rms(x_ref[...], g_ref[...]).astype(BF16)
    _glu_project(h, w_ref, u_ref, 0)
    for task in _qkv_gate_tasks(h, w_ref, qkv_ref, sg_ref):
        task()


def _inproj(layer, x, g, w_in, tm):
    rows = x.shape[0]
    return pl.pallas_call(
        _inproj_body,
        grid=(rows // tm,),
        in_specs=[
            pl.BlockSpec((tm, D_MODEL), lambda i: (i, 0)),
            _layer_spec(g, layer),
            _layer_spec(w_in, layer),
        ],
        out_specs=[
            pl.BlockSpec((tm, D_CONV), lambda i: (i, 0)),
            pl.BlockSpec((tm, 3 * D_ATTN), lambda i: (i, 0)),
            pl.BlockSpec((tm, 2 * D_MODEL), lambda i: (i, 0)),
        ],
        out_shape=[
            jax.ShapeDtypeStruct((rows, D_CONV), F32),
            jax.ShapeDtypeStruct((rows, 3 * D_ATTN), F32),
            jax.ShapeDtypeStruct((rows, 2 * D_MODEL), F32),
        ],
        compiler_params=pltpu.CompilerParams(
            dimension_semantics=("parallel",), vmem_limit_bytes=VMEM_LIMIT),
        name="inproj",
    )(x, g, w_in)


def _ln_swish(y, g, b):
    mu = jnp.mean(y, axis=-1, keepdims=True)
    yc = y - mu
    var = jnp.mean(yc * yc, axis=-1, keepdims=True)
    z = yc * lax.rsqrt(var + EPS) * g + b
    return z * jax.nn.sigmoid(z)


def _shifted_rows(blk, s):
    if s == 0:
        return blk[:-1]
    rot = pltpu.roll(blk, SUBLANES - s, axis=1)
    sub = lax.broadcasted_iota(jnp.int32, rot[:-1].shape, 1)
    return jnp.where(sub < SUBLANES - s, rot[:-1], rot[1:])


def _conv31_lanes(buf, w8_ref, b_ref, ybuf, rows, l0):
    first = CONV_HALO - (CONV_WIDTH - 1)
    rc = CONV_ROW_CHUNK
    nv = rc // SUBLANES
    lanes = slice(l0, l0 + LANES)

    def row_chunk(i, carry):
        r0 = pl.multiple_of(i * rc, rc)
        blk = buf[pl.ds(r0, rc + CONV_HALO + SUBLANES), lanes].reshape(nv + 5, SUBLANES, LANES)
        acc = jnp.broadcast_to(b_ref[:, lanes], (nv, SUBLANES, LANES))
        for s in range(SUBLANES):
            sh = _shifted_rows(blk, s)
            for a in range(5):
                k = SUBLANES * a + s - first
                if 0 <= k < CONV_WIDTH:
                    acc = acc + w8_ref[k, :, lanes] * sh[a:a + nv]
        ybuf[pl.ds(r0, rc), lanes] = acc.reshape(rc, LANES)
        return carry

    lax.fori_loop(0, rows // rc, row_chunk, 0)


def _conv_prompt_body(tm, u_ref, cw8_ref, cb_ref, lng_ref, lnb_ref, c_ref, ubuf, ybuf):
    @pl.when(pl.program_id(0) == 0)
    def _():
        ubuf[0:CONV_HALO, :] = jnp.zeros((CONV_HALO, D_CONV), F32)
        ubuf[CONV_HALO + tm:, :] = jnp.zeros((SUBLANES, D_CONV), F32)

    ubuf[CONV_HALO:CONV_HALO + tm, :] = u_ref[...]
    for l0 in range(0, D_CONV, LANES):
        _conv31_lanes(ubuf, cw8_ref, cb_ref, ybuf, tm, l0)
    c_ref[...] = _ln_swish(ybuf[...], lng_ref[...], lnb_ref[...]).astype(c_ref.dtype)
    ubuf[0:CONV_HALO, :] = ubuf[tm:tm + CONV_HALO, :]


def _conv_prompt(layer, u, cw8, cb, lng, lnb, tm):
    rows = u.shape[0]
    return pl.pallas_call(
        functools.partial(_conv_prompt_body, tm),
        grid=(rows // tm,),
        in_specs=[
            pl.BlockSpec((tm, D_CONV), lambda i: (i, 0)),
            _layer_spec(cw8, layer),
            _layer_spec(cb, layer),
            _layer_spec(lng, layer),
            _layer_spec(lnb, layer),
        ],
        out_specs=pl.BlockSpec((tm, D_CONV), lambda i: (i, 0)),
        out_shape=jax.ShapeDtypeStruct((rows, D_CONV), BF16),
        scratch_shapes=[
            pltpu.VMEM((CONV_HALO + tm + SUBLANES, D_CONV), F32),
            pltpu.VMEM((tm, D_CONV), F32),
        ],
        compiler_params=pltpu.CompilerParams(
            dimension_semantics=("arbitrary",), vmem_limit_bytes=VMEM_LIMIT),
        name="conv_prompt",
    )(u, cw8, cb, lng, lnb)


def _pair_attention(q2, k2, v2, mask):
    m_rows = q2.shape[0]
    lane = lax.broadcasted_iota(jnp.int32, q2.shape, 1)
    zero = jnp.zeros_like(q2)
    qq = jnp.concatenate([jnp.where(lane < HEAD_DIM, q2, zero), jnp.where(lane >= HEAD_DIM, q2, zero)], axis=0)
    s = lax.dot_general(qq.astype(BF16), k2.astype(BF16), (((1,), (1,)), ((), ())), preferred_element_type=F32)
    s = jnp.where(mask, s, -jnp.inf)
    mx = jnp.max(s, axis=-1, keepdims=True)
    p = jnp.exp(s - mx)
    l = jnp.sum(p, axis=-1, keepdims=True)
    pv = jnp.dot(p.astype(BF16), v2.astype(BF16), preferred_element_type=F32)
    o = pv / l
    lse = jnp.broadcast_to(mx + jnp.log(l), o.shape)
    first = lane < HEAD_DIM
    return (jnp.where(first, o[:m_rows], o[m_rows:]),
            jnp.where(first, lse[:m_rows], lse[m_rows:]))


def _pair_attention_cached(q2, kt, vt, kn, vn, mask_c, mask_n):
    m_rows = q2.shape[0]
    lane = lax.broadcasted_iota(jnp.int32, q2.shape, 1)
    zero = jnp.zeros_like(q2)
    qq = jnp.concatenate([jnp.where(lane < HEAD_DIM, q2, zero), jnp.where(lane >= HEAD_DIM, q2, zero)],
                         axis=0).astype(BF16)
    nt = (((1,), (1,)), ((), ()))
    s_c = jnp.where(mask_c, jnp.dot(qq, kt.astype(BF16), preferred_element_type=F32), -jnp.inf)
    s_n = jnp.where(mask_n, lax.dot_general(qq, kn.astype(BF16), nt, preferred_element_type=F32), -jnp.inf)
    mx = jnp.maximum(jnp.max(s_c, axis=-1, keepdims=True), jnp.max(s_n, axis=-1, keepdims=True))
    p_c = jnp.exp(s_c - mx)
    p_n = jnp.exp(s_n - mx)
    l = jnp.sum(p_c, axis=-1, keepdims=True) + jnp.sum(p_n, axis=-1, keepdims=True)
    pv = (lax.dot_general(p_c.astype(BF16), vt.astype(BF16), nt, preferred_element_type=F32)
          + jnp.dot(p_n.astype(BF16), vn.astype(BF16), preferred_element_type=F32))
    o = pv / l
    lse = jnp.broadcast_to(mx + jnp.log(l), o.shape)
    first = lane < HEAD_DIM
    return (jnp.where(first, o[:m_rows], o[m_rows:]),
            jnp.where(first, lse[:m_rows], lse[m_rows:]))


def _attn_prompt_body(d, q_ref, k_ref, v_ref, o_ref, lse_ref, qs, kbuf, vbuf, os_, ls_):
    tq = ATTN_TILE
    blk = ATTN_BLOCK
    span = blk * d
    i = pl.program_id(0)

    @pl.when(i == 0)
    def _():
        kbuf[:, 0:tq, :] = jnp.zeros((2, tq, LANES), F32)
        vbuf[:, 0:tq, :] = jnp.zeros((2, tq, LANES), F32)

    for hp in range(2):
        cols = slice(hp * LANES, (hp + 1) * LANES)
        qs[hp] = q_ref[:, cols] * SCALE
        kbuf[hp, tq:2 * tq, :] = k_ref[:, cols]
        vbuf[hp, tq:2 * tq, :] = v_ref[:, cols]

    row = lax.broadcasted_iota(jnp.int32, (2 * blk, 2 * blk), 0) & (blk - 1)
    col = lax.broadcasted_iota(jnp.int32, (2 * blk, 2 * blk), 1)
    band = (col >= row) & (col <= row + (N_KEYS - 1))

    def problem(j, carry):
        b = j // d
        r = j % d
        qstart = b * span + r
        kstart = tq + (b - 1) * span + r
        cmin = jnp.where((i == 0) & (b == 0), blk, 0)
        mask = band & (col >= cmin)
        for hp in range(2):
            if d == 1:
                qidx = pl.ds(pl.multiple_of(qstart, blk), blk)
                kidx = pl.ds(pl.multiple_of(kstart, blk), 2 * blk)
            else:
                qidx = pl.ds(qstart, blk, stride=d)
                kidx = pl.ds(kstart, 2 * blk, stride=d)
            o2, lse2 = _pair_attention(qs[hp, qidx, :], kbuf[hp, kidx, :], vbuf[hp, kidx, :], mask)
            os_[hp, qidx, :] = o2
            ls_[hp, qidx, :] = lse2
        return carry

    lax.fori_loop(0, tq // blk, problem, 0, unroll=4)

    for hp in range(2):
        cols = slice(hp * LANES, (hp + 1) * LANES)
        o_ref[:, cols] = os_[hp]
        lse_ref[:, cols] = ls_[hp]
    kbuf[:, 0:tq, :] = kbuf[:, tq:2 * tq, :]
    vbuf[:, 0:tq, :] = vbuf[:, tq:2 * tq, :]


def _attn_prompt(qkv, g, d):
    rows = qkv.shape[0]
    tq = ATTN_TILE
    n_grp = len(ATTN_GROUPS)
    slab = pltpu.VMEM((2, tq, LANES), F32)
    slab2 = pltpu.VMEM((2, 2 * tq, LANES), F32)
    return pl.pallas_call(
        functools.partial(_attn_prompt_body, d),
        grid=(rows // tq,),
        in_specs=[
            pl.BlockSpec((tq, D_GROUP), lambda i: (i, g)),
            pl.BlockSpec((tq, D_GROUP), lambda i: (i, n_grp + g)),
            pl.BlockSpec((tq, D_GROUP), lambda i: (i, 2 * n_grp + g)),
        ],
        out_specs=[
            pl.BlockSpec((tq, D_GROUP), lambda i: (i, 0)),
            pl.BlockSpec((tq, D_GROUP), lambda i: (i, 0)),
        ],
        out_shape=[
            jax.ShapeDtypeStruct((rows, D_GROUP), F32),
            jax.ShapeDtypeStruct((rows, D_GROUP), F32),
        ],
        scratch_shapes=[slab, slab2, slab2, slab, slab],
        compiler_params=pltpu.CompilerParams(
            dimension_semantics=("arbitrary",), vmem_limit_bytes=VMEM_LIMIT),
        name=f"attn_prompt_d{d}",
    )(qkv, qkv, qkv)


def _mixer_sample_body(t_new, hist_lens, *refs):
    n_grp = len(ATTN_GROUPS)
    (u_ref, qkv_ref, hist_ref, w_ref, b_ref, lng_ref, lnb_ref) = refs[:7]
    cache_refs = refs[7:7 + 2 * n_grp]
    pos = 7 + 2 * n_grp
    c_ref, state_ref = refs[pos:pos + 2]
    ol_refs = refs[pos + 2:pos + 2 + 2 * n_grp]
    new_cache_refs = refs[pos + 2 + 2 * n_grp:pos + 2 + 4 * n_grp]
    xs = refs[pos + 2 + 4 * n_grp]

    n_hist = CONV_WIDTH - 1
    xs[0:n_hist, :] = hist_ref[...]
    xs[n_hist:n_hist + t_new, :] = u_ref[...]
    acc = jnp.broadcast_to(b_ref[...], (t_new, D_CONV))
    for k in range(CONV_WIDTH):
        acc = acc + w_ref[k:k + 1, :] * xs[k:k + t_new, :]
    c_ref[...] = _ln_swish(acc, lng_ref[...], lnb_ref[...])
    state_ref[...] = xs[t_new:t_new + n_hist, :]

    def window_mask(delta, d):
        return (delta >= 0) & ((delta & (d - 1)) == 0) & (delta <= (N_KEYS - 1) * d)

    pad_rows = jnp.zeros((LANES - t_new, D_GROUP), F32)
    t_new_idx = lax.broadcasted_iota(jnp.int32, (2 * t_new, LANES), 0) & (t_new - 1)
    j_new_idx = lax.broadcasted_iota(jnp.int32, (2 * t_new, LANES), 1)
    lane = lax.broadcasted_iota(jnp.int32, (D_GROUP, LANES), 1)
    for g, (_, d) in enumerate(ATTN_GROUPS):
        hl = hist_lens[g]
        kt_ref, vt_ref = cache_refs[2 * g], cache_refs[2 * g + 1]
        kn = jnp.concatenate([qkv_ref[:, D_ATTN + g * D_GROUP:D_ATTN + (g + 1) * D_GROUP], pad_rows], axis=0)
        vn = jnp.concatenate([qkv_ref[:, 2 * D_ATTN + g * D_GROUP:2 * D_ATTN + (g + 1) * D_GROUP], pad_rows], axis=0)

        t_idx = lax.broadcasted_iota(jnp.int32, (2 * t_new, hl), 0) & (t_new - 1)
        c_idx = lax.broadcasted_iota(jnp.int32, (2 * t_new, hl), 1)
        mask_c = window_mask(hl + t_idx - c_idx, d)
        mask_n = window_mask(t_new_idx - j_new_idx, d) & (j_new_idx < t_new)
        for hp in range(2):
            cols = slice(hp * LANES, (hp + 1) * LANES)
            q2 = qkv_ref[:, g * D_GROUP + hp * LANES:g * D_GROUP + (hp + 1) * LANES] * SCALE
            o2, lse2 = _pair_attention_cached(q2, kt_ref[cols, :], vt_ref[cols, :], kn[:, cols], vn[:, cols],
                                              mask_c, mask_n)
            ol_refs[2 * g][:, cols] = o2
            ol_refs[2 * g + 1][:, cols] = lse2

        for old_ref, new_ref, rows_new in ((kt_ref, new_cache_refs[2 * g], kn), (vt_ref, new_cache_refs[2 * g + 1], vn)):
            placed = pltpu.roll(rows_new.T, LANES - t_new, axis=1)
            shifted = pltpu.roll(old_ref[...], hl - t_new, axis=1)
            if hl > LANES:
                new_ref[:, 0:hl - LANES] = shifted[:, 0:hl - LANES]
            new_ref[:, hl - LANES:hl] = jnp.where(lane >= LANES - t_new, placed, shifted[:, hl - LANES:hl])


def _mixer_sample(layer, u, qkv, state_conv, caches, prev_new_caches, w, b, lng, lnb):
    n_grp = len(ATTN_GROUPS)
    depth, n_seq = caches[0].shape[:2]
    t_new = u.shape[0] // n_seq
    hist_lens = tuple(c.shape[3] for c in caches[0::2])
    n_hist = CONV_WIDTH - 1

    in_specs = [
        pl.BlockSpec((t_new, D_CONV), lambda n: (n, 0)),
        pl.BlockSpec((t_new, 3 * D_ATTN), lambda n: (n, 0)),
        pl.BlockSpec((None, None, n_hist, D_CONV), lambda n: (layer, n, 0, 0)),
        _layer_spec(w, layer),
        _layer_spec(b, layer),
        _layer_spec(lng, layer),
        _layer_spec(lnb, layer),
    ]
    for g in range(n_grp):
        for _ in range(2):
            in_specs.append(pl.BlockSpec((None, None, D_GROUP, hist_lens[g]), lambda n: (layer, n, 0, 0)))
    args = [u, qkv, state_conv, w, b, lng, lnb, *caches]
    aliases = {}
    n_fixed_out = 2 + 2 * n_grp
    if prev_new_caches is not None:
        for j, prev in enumerate(prev_new_caches):
            in_specs.append(pl.BlockSpec(memory_space=pl.ANY))
            aliases[len(args)] = n_fixed_out + j
            args.append(prev)

    out_specs = [
        pl.BlockSpec((t_new, D_CONV), lambda n: (n, 0)),
        pl.BlockSpec((None, n_hist, D_CONV), lambda n: (n, 0, 0)),
    ]
    out_shape = [
        jax.ShapeDtypeStruct((n_seq * t_new, D_CONV), F32),
        jax.ShapeDtypeStruct((n_seq, n_hist, D_CONV), F32),
    ]
    for g in range(n_grp):
        for _ in range(2):
            out_specs.append(pl.BlockSpec((t_new, D_GROUP), lambda n: (n, 0)))
            out_shape.append(jax.ShapeDtypeStruct((n_seq * t_new, D_GROUP), F32))
    for g in range(n_grp):
        for _ in range(2):
            out_specs.append(pl.BlockSpec((None, None, D_GROUP, hist_lens[g]), lambda n: (layer, n, 0, 0)))
            out_shape.append(jax.ShapeDtypeStruct((depth, n_seq, D_GROUP, hist_lens[g]), F32))

    scratch = [pltpu.VMEM((n_hist + t_new + 2, D_CONV), F32)]

    def body(*refs):
        if prev_new_caches is not None:
            k = len(prev_new_caches)
            n_in = 7 + 2 * n_grp
            refs = refs[:n_in] + refs[n_in + k:]
        _mixer_sample_body(t_new, hist_lens, *refs)

    outs = pl.pallas_call(
        body,
        grid=(n_seq,),
        in_specs=in_specs,
        out_specs=out_specs,
        out_shape=out_shape,
        scratch_shapes=scratch,
        input_output_aliases=aliases,
        compiler_params=pltpu.CompilerParams(
            dimension_semantics=("arbitrary",), vmem_limit_bytes=VMEM_LIMIT),
        name=f"mixer_sample_l{layer}",
    )(*args)
    c, state = outs[0], outs[1]
    ol = outs[2:2 + 2 * n_grp]
    new_caches = outs[2 + 2 * n_grp:]
    return c, state, ol, new_caches


FF_CHUNK = 512


def _post_body(tm, per_seq, final, *refs):
    n_grp = len(ATTN_GROUPS)
    x_ref, c_ref = refs[0:2]
    ol_refs = refs[2:2 + 2 * n_grp]
    pos = 2 + 2 * n_grp
    sg_ref = refs[pos]
    (wco_ref, wao_ref, wo_ref, g2_ref, wup_ref, fw_ref, fb_ref, wdn_ref) = refs[pos + 1:pos + 9]
    pos += 9
    if final:
        gf_ref = refs[pos]
        pos += 1
    if per_seq:
        h0_ref, h1_ref = refs[pos:pos + 2]
        pos += 2
    y_ref, up_ref = refs[pos:pos + 2]
    carry = None if per_seq else refs[pos + 2]

    lses = [ol_refs[2 * g + 1][...] for g in range(n_grp)]
    lmax = functools.reduce(jnp.maximum, lses)
    es = [jnp.exp(l - lmax) for l in lses]
    den = functools.reduce(lambda a, b: a + b, es)
    attn = functools.reduce(lambda a, b: a + b, [(es[g] / den) * ol_refs[2 * g][...] for g in range(n_grp)])

    branch_conv = jnp.dot(c_ref[...].astype(BF16), wco_ref[...], preferred_element_type=F32)
    branch_attn = jnp.dot(attn.astype(BF16), wao_ref[...], preferred_element_type=F32)
    mixed = sg_ref[:, 0:D_MODEL] * branch_conv + sg_ref[:, D_MODEL:2 * D_MODEL] * branch_attn
    x1 = x_ref[...] + jnp.dot(mixed.astype(BF16), wo_ref[...], preferred_element_type=F32)

    h = _rms(x1, g2_ref[...]).astype(BF16)
    if not per_seq:
        @pl.when(pl.program_id(0) == 0)
        def _():
            carry[...] = jnp.zeros(carry.shape, F32)
    out_rows = up_ref.shape[0]
    t_in_seq = lax.broadcasted_iota(jnp.int32, (tm, FF_CHUNK), 0) & (SUBLANES - 1)

    def conv3(c0):
        cols = slice(c0, c0 + FF_CHUNK)
        up = jnp.dot(h, wup_ref[:, cols], preferred_element_type=F32)
        up_ref[:, cols] = up[tm - out_rows:, :]
        head = up[0:SUBLANES, :] if per_seq else carry[:, cols]
        ext = jnp.concatenate([head, up], axis=0).reshape(tm // SUBLANES + 1, SUBLANES, FF_CHUNK)
        p1 = _shifted_rows(ext, SUBLANES - 1).reshape(tm, FF_CHUNK)
        p2 = _shifted_rows(ext, SUBLANES - 2).reshape(tm, FF_CHUNK)
        if per_seq:
            p1 = jnp.where(t_in_seq == 0, h1_ref[:, cols], p1)
            p2 = jnp.where(t_in_seq == 0, h0_ref[:, cols], jnp.where(t_in_seq == 1, h1_ref[:, cols], p2))
        else:
            carry[:, cols] = up[tm - SUBLANES:, :]
        return fw_ref[0:1, cols] * p2 + fw_ref[1:2, cols] * p1 + fw_ref[2:3, cols] * up + fb_ref[:, cols]

    acc = jnp.zeros((tm, D_MODEL), F32)
    for c0 in range(0, D_FF, FF_CHUNK):
        val = conv3(c0)
        gate = conv3(D_FF + c0)
        act = (jax.nn.silu(gate) * val).astype(BF16)
        acc = acc + jnp.dot(act, wdn_ref[c0:c0 + FF_CHUNK, :], preferred_element_type=F32)
    x2 = x1 + acc
    if final:
        x2 = _rms(x2, gf_ref[...])
    y_ref[...] = x2


def _post(layer, x, c, ol, sg, wco, wao, wo, g2, wup, fw, fb, wdn, gf, hist, tm):
    rows = x.shape[0]
    n_grp = len(ATTN_GROUPS)
    per_seq = hist is not None
    final = gf is not None
    row_spec = lambda w: pl.BlockSpec((tm, w), lambda i: (i, 0))
    in_specs = [row_spec(D_MODEL), row_spec(D_CONV)] + [row_spec(D_GROUP)] * (2 * n_grp) + [row_spec(2 * D_MODEL)]
    args = [x, c, *ol, sg]
    for a in (wco, wao, wo, g2, wup, fw, fb, wdn):
        in_specs.append(_layer_spec(a, layer))
        args.append(a)
    if final:
        in_specs.append(_const_spec(gf.shape))
        args.append(gf)
    if per_seq:
        in_specs += [row_spec(2 * D_FF)] * 2
        args += list(hist)
    up_rows = rows if per_seq else SUBLANES
    up_spec = row_spec(2 * D_FF) if per_seq else pl.BlockSpec((SUBLANES, 2 * D_FF), lambda i: (0, 0))
    scratch = [] if per_seq else [pltpu.VMEM((SUBLANES, 2 * D_FF), F32)]
    return pl.pallas_call(
        functools.partial(_post_body, tm, per_seq, final),
        grid=(rows // tm,),
        in_specs=in_specs,
        out_specs=[row_spec(D_MODEL), up_spec],
        out_shape=[
            jax.ShapeDtypeStruct((rows, D_MODEL), F32),
            jax.ShapeDtypeStruct((up_rows, 2 * D_FF), F32),
        ],
        scratch_shapes=scratch,
        compiler_params=pltpu.CompilerParams(
            dimension_semantics=("arbitrary",), vmem_limit_bytes=VMEM_LIMIT),
        name="post_sample" if per_seq else "post_prompt",
    )(*args)


def kernel(x_prompt, x_sample, state_conv, cache_k_w128, cache_v_w128, cache_k_w512, cache_v_w512, cache_k_w2048, cache_v_w2048, state_ffn_conv, norm_attn_g, w_in, conv_dw_w, conv_dw_b, conv_ln_g, conv_ln_b, w_conv_out, w_attn_out, w_out, norm_ffn_g, w_up, ffn_dw_w, ffn_dw_b, w_down, norm_final_g):
    depth = w_in.shape[0]
    n_b, seq, _ = x_prompt.shape
    n_s, t_new, _ = x_sample.shape
    assert n_b == 1 and seq % ATTN_TILE == 0 and seq % ROW_TILE == 0 and ROW_TILE % CONV_ROW_CHUNK == 0
    assert t_new == SUBLANES
    n_grp = len(ATTN_GROUPS)
    caches = [jnp.transpose(c, (0, 1, 3, 4, 2)).reshape(c.shape[:2] + (D_GROUP, c.shape[2])) for c in
              (cache_k_w128, cache_v_w128, cache_k_w512, cache_v_w512, cache_k_w2048, cache_v_w2048)]

    xp = x_prompt.reshape(seq, D_MODEL)
    xs = x_sample.reshape(n_s * t_new, D_MODEL)
    rows3 = lambda a: a.reshape(depth, 1, -1)
    gf = norm_final_g.reshape(1, -1)
    w_in_b = w_in.astype(BF16)
    wco, wao, wo = w_conv_out.astype(BF16), w_attn_out.astype(BF16), w_out.astype(BF16)
    wup, wdn = w_up.astype(BF16), w_down.astype(BF16)
    g1, g2 = rows3(norm_attn_g), rows3(norm_ffn_g)
    cw, cb, lng, lnb = conv_dw_w, rows3(conv_dw_b), rows3(conv_ln_g), rows3(conv_ln_b)
    cw8 = jnp.broadcast_to(cw[:, :, None, :], (depth, CONV_WIDTH, SUBLANES, D_CONV))
    fw, fb = ffn_dw_w, rows3(ffn_dw_b)

    conv_p, conv_s, ffn_p, ffn_s = [], [], [], []
    kv_p = [[[], []] for _ in range(n_grp)]
    new_caches = None
    for l in range(depth):
        last = l == depth - 1
        post_w = (wco, wao, wo, g2, wup, fw, fb, wdn, gf if last else None)

        u, qkv, sg = _inproj(l, xp, g1, w_in_b, ROW_TILE)
        c = _conv_prompt(l, u, cw8, cb, lng, lnb, ROW_TILE)
        ol = []
        for g, (_, d) in enumerate(ATTN_GROUPS):
            ol += _attn_prompt(qkv, g, d)
        xp, up_tail = _post(l, xp, c, ol, sg, *post_w, None, ROW_TILE)
        conv_p.append(u[seq - (CONV_WIDTH - 1):].reshape(1, CONV_WIDTH - 1, D_CONV))
        ffn_p.append(up_tail[SUBLANES - (FFN_CONV_WIDTH - 1):].reshape(1, FFN_CONV_WIDTH - 1, 2 * D_FF))
        for g, (window, _) in enumerate(ATTN_GROUPS):
            keep = min(window, seq)
            for j in range(2):
                c0 = (1 + j) * D_ATTN + g * D_GROUP
                kv_p[g][j].append(qkv[seq - keep:, c0:c0 + D_GROUP].reshape(1, keep, HEADS_PER_GROUP, HEAD_DIM))

        us, qkvs, sgs = _inproj(l, xs, g1, w_in_b, n_s * t_new)
        cs, state_s, ols, new_caches = _mixer_sample(l, us, qkvs, state_conv, caches, new_caches, cw, cb, lng, lnb)
        hist = tuple(jnp.repeat(state_ffn_conv[l][:, j], t_new, axis=0) for j in range(FFN_CONV_WIDTH - 1))
        xs, up_s = _post(l, xs, cs, ols, sgs, *post_w, hist, n_s * t_new)
        conv_s.append(state_s)
        ffn_s.append(up_s.reshape(n_s, t_new, 2 * D_FF)[:, t_new - (FFN_CONV_WIDTH - 1):])

    st = lambda lst: jnp.stack(lst, axis=0)
    kv_s = [jnp.transpose(nc.reshape(nc.shape[:2] + (HEADS_PER_GROUP, HEAD_DIM, nc.shape[3])), (0, 1, 4, 2, 3))
            for nc in new_caches]
    return (xp.reshape(n_b, seq, D_MODEL), xs.reshape(n_s, t_new, D_MODEL),
            st(conv_p), st(conv_s),
            st(kv_p[0][0]), st(kv_p[0][1]), kv_s[0], kv_s[1],
            st(kv_p[1][0]), st(kv_p[1][1]), kv_s[2], kv_s[3],
            st(kv_p[2][0]), st(kv_p[2][1]), kv_s[4], kv_s[5],
            st(ffn_p), st(ffn_s))
```

```python
import functools

import jax
import jax.numpy as jnp
from jax import lax
from jax.experimental import pallas as pl
from jax.experimental.pallas import tpu as pltpu

F32 = jnp.float32
BF16 = jnp.bfloat16

D_MODEL = 1024
D_CONV = 1024
CONV_WIDTH = 31
HEAD_DIM = 64
HEADS_PER_GROUP = 4
ATTN_GROUPS = ((128, 1), (512, 4), (2048, 16))
N_KEYS = 129
D_GROUP = HEADS_PER_GROUP * HEAD_DIM
D_ATTN = len(ATTN_GROUPS) * D_GROUP
D_FF = 3 * D_MODEL
FFN_CONV_WIDTH = 3
EPS = 1e-6
SCALE = HEAD_DIM ** -0.5

LANES = 128
SUBLANES = 8
VMEM_LIMIT = 60 * 1024 * 1024

ROW_TILE = 512
CONV_HALO = 32
CONV_ROW_CHUNK = 256
ATTN_TILE = 2048
ATTN_BLOCK = 128


def _const_spec(shape):
    nd = len(shape)
    return pl.BlockSpec(shape, lambda *_: (0,) * nd, pipeline_mode=pl.Buffered(1))


def _layer_spec(stacked, layer):
    rest = stacked.shape[1:]
    return pl.BlockSpec((None,) + rest, lambda *_: (layer,) + (0,) * len(rest), pipeline_mode=pl.Buffered(1))


def _rms(x, g):
    return x * lax.rsqrt(jnp.mean(x * x, axis=-1, keepdims=True) + EPS) * g


PROJ_CHUNK = 512


def _glu_project(h, w_ref, u_ref, row0):
    rows = h.shape[0]
    for j in range(0, D_CONV, PROJ_CHUNK):
        a = jnp.dot(h, w_ref[:, j:j + PROJ_CHUNK], preferred_element_type=F32)
        b = jnp.dot(h, w_ref[:, D_CONV + j:D_CONV + j + PROJ_CHUNK], preferred_element_type=F32)
        u_ref[row0:row0 + rows, j:j + PROJ_CHUNK] = a * jax.nn.sigmoid(b)


def _qkv_gate_tasks(h, w_ref, qkv_ref, sg_ref):
    tasks = []
    q0 = 2 * D_CONV
    g0 = q0 + 3 * D_ATTN

    def qkv_task(j, width):
        def run():
            qkv_ref[:, j:j + width] = jnp.dot(h, w_ref[:, q0 + j:q0 + j + width], preferred_element_type=F32)
        return run

    def gate_task(j):
        def run():
            sg_ref[:, j:j + PROJ_CHUNK] = jax.nn.sigmoid(
                jnp.dot(h, w_ref[:, g0 + j:g0 + j + PROJ_CHUNK], preferred_element_type=F32))
        return run

    for j in range(0, 3 * D_ATTN, PROJ_CHUNK):
        tasks.append(qkv_task(j, min(PROJ_CHUNK, 3 * D_ATTN - j)))
    for j in range(0, 2 * D_MODEL, PROJ_CHUNK):
        tasks.append(gate_task(j))
    return tasks


def _inproj_body(x_ref, g_ref, w_ref, u_ref, qkv_ref, sg_ref):
    h = _rms(x_ref[...], g_ref[...]).astype(BF16)
    _glu_project(h, w_ref, u_ref, 0)
    for task in _qkv_gate_tasks(h, w_ref, qkv_ref, sg_ref):
        task()


def _inproj(layer, x, g, w_in, tm):
    rows = x.shape[0]
    return pl.pallas_call(
        _inproj_body,
        grid=(rows // tm,),
        in_specs=[
            pl.BlockSpec((tm, D_MODEL), lambda i: (i, 0)),
            _layer_spec(g, layer),
            _layer_spec(w_in, layer),
        ],
        out_specs=[
            pl.BlockSpec((tm, D_CONV), lambda i: (i, 0)),
            pl.BlockSpec((tm, 3 * D_ATTN), lambda i: (i, 0)),
            pl.BlockSpec((tm, 2 * D_MODEL), lambda i: (i, 0)),
        ],
        out_shape=[
            jax.ShapeDtypeStruct((rows, D_CONV), F32),
            jax.ShapeDtypeStruct((rows, 3 * D_ATTN), F32),
            jax.ShapeDtypeStruct((rows, 2 * D_MODEL), F32),
        ],
        compiler_params=pltpu.CompilerParams(
            dimension_semantics=("parallel",), vmem_limit_bytes=VMEM_LIMIT),
        name="inproj",
    )(x, g, w_in)


def _ln_swish(y, g, b):
    mu = jnp.mean(y, axis=-1, keepdims=True)
    yc = y - mu
    var = jnp.mean(yc * yc, axis=-1, keepdims=True)
    z = yc * lax.rsqrt(var + EPS) * g + b
    return z * jax.nn.sigmoid(z)


def _shifted_rows(blk, s):
    if s == 0:
        return blk[:-1]
    rot = pltpu.roll(blk, SUBLANES - s, axis=1)
    sub = lax.broadcasted_iota(jnp.int32, rot[:-1].shape, 1)
    return jnp.where(sub < SUBLANES - s, rot[:-1], rot[1:])


def _conv31_lanes(buf, w8_ref, b_ref, ybuf, rows, l0):
    first = CONV_HALO - (CONV_WIDTH - 1)
    rc = CONV_ROW_CHUNK
    nv = rc // SUBLANES
    lanes = slice(l0, l0 + LANES)

    def row_chunk(i, carry):
        r0 = pl.multiple_of(i * rc, rc)
        blk = buf[pl.ds(r0, rc + CONV_HALO + SUBLANES), lanes].reshape(nv + 5, SUBLANES, LANES)
        acc = jnp.broadcast_to(b_ref[:, lanes], (nv, SUBLANES, LANES))
        for s in range(SUBLANES):
            sh = _shifted_rows(blk, s)
            for a in range(5):
                k = SUBLANES * a + s - first
                if 0 <= k < CONV_WIDTH:
                    acc = acc + w8_ref[k, :, lanes] * sh[a:a + nv]
        ybuf[pl.ds(r0, rc), lanes] = acc.reshape(rc, LANES)
        return carry

    lax.fori_loop(0, rows // rc, row_chunk, 0)


def _conv_prompt_body(tm, u_ref, cw8_ref, cb_ref, y_ref, ubuf):
    @pl.when(pl.program_id(0) == 0)
    def _():
        ubuf[0:CONV_HALO, :] = jnp.zeros((CONV_HALO, D_CONV), F32)
        ubuf[CONV_HALO + tm:, :] = jnp.zeros((SUBLANES, D_CONV), F32)

    ubuf[CONV_HALO:CONV_HALO + tm, :] = u_ref[...]
    for l0 in range(0, D_CONV, LANES):
        _conv31_lanes(ubuf, cw8_ref, cb_ref, y_ref, tm, l0)
    ubuf[0:CONV_HALO, :] = ubuf[tm:tm + CONV_HALO, :]


def _conv_prompt(layer, u, cw8, cb, tm):
    rows = u.shape[0]
    return pl.pallas_call(
        functools.partial(_conv_prompt_body, tm),
        grid=(rows // tm,),
        in_specs=[
            pl.BlockSpec((tm, D_CONV), lambda i: (i, 0)),
            _layer_spec(cw8, layer),
            _layer_spec(cb, layer),
        ],
        out_specs=pl.BlockSpec((tm, D_CONV), lambda i: (i, 0)),
        out_shape=jax.ShapeDtypeStruct((rows, D_CONV), F32),
        scratch_shapes=[pltpu.VMEM((CONV_HALO + tm + SUBLANES, D_CONV), F32)],
        compiler_params=pltpu.CompilerParams(
            dimension_semantics=("arbitrary",), vmem_limit_bytes=VMEM_LIMIT),
        name="conv_prompt",
    )(u, cw8, cb)


def _pair_attention(q2, k2, v2, mask):
    m_rows = q2.shape[0]
    lane = lax.broadcasted_iota(jnp.int32, q2.shape, 1)
    zero = jnp.zeros_like(q2)
    qq = jnp.concatenate([jnp.where(lane < HEAD_DIM, q2, zero), jnp.where(lane >= HEAD_DIM, q2, zero)], axis=0)
    s = lax.dot_general(qq.astype(BF16), k2.astype(BF16), (((1,), (1,)), ((), ())), preferred_element_type=F32)
    s = jnp.where(mask, s, -jnp.inf)
    mx = jnp.max(s, axis=-1, keepdims=True)
    p = jnp.exp(s - mx)
    l = jnp.sum(p, axis=-1, keepdims=True)
    pv = jnp.dot(p.astype(BF16), v2.astype(BF16), preferred_element_type=F32)
    o = pv / l
    lse = jnp.broadcast_to(mx + jnp.log(l), o.shape)
    first = lane < HEAD_DIM
    return (jnp.where(first, o[:m_rows], o[m_rows:]),
            jnp.where(first, lse[:m_rows], lse[m_rows:]))


def _pair_attention_cached(q2, kt, vt, kn, vn, mask_c, mask_n):
    m_rows = q2.shape[0]
    lane = lax.broadcasted_iota(jnp.int32, q2.shape, 1)
    zero = jnp.zeros_like(q2)
    qq = jnp.concatenate([jnp.where(lane < HEAD_DIM, q2, zero), jnp.where(lane >= HEAD_DIM, q2, zero)],
                         axis=0).astype(BF16)
    nt = (((1,), (1,)), ((), ()))
    s_c = jnp.where(mask_c, jnp.dot(qq, kt.astype(BF16), preferred_element_type=F32), -jnp.inf)
    s_n = jnp.where(mask_n, lax.dot_general(qq, kn.astype(BF16), nt, preferred_element_type=F32), -jnp.inf)
    mx = jnp.maximum(jnp.max(s_c, axis=-1, keepdims=True), jnp.max(s_n, axis=-1, keepdims=True))
    p_c = jnp.exp(s_c - mx)
    p_n = jnp.exp(s_n - mx)
    l = jnp.sum(p_c, axis=-1, keepdims=True) + jnp.sum(p_n, axis=-1, keepdims=True)
    pv = (lax.dot_general(p_c.astype(BF16), vt.astype(BF16), nt, preferred_element_type=F32)
          + jnp.dot(p_n.astype(BF16), vn.astype(BF16), preferred_element_type=F32))
    o = pv / l
    lse = jnp.broadcast_to(mx + jnp.log(l), o.shape)
    first = lane < HEAD_DIM
    return (jnp.where(first, o[:m_rows], o[m_rows:]),
            jnp.where(first, lse[:m_rows], lse[m_rows:]))


def _attn_prompt_body(d, q_ref, k_ref, v_ref, o_ref, lse_ref, qs, kbuf, vbuf, os_, ls_):
    tq = ATTN_TILE
    blk = ATTN_BLOCK
    span = blk * d
    i = pl.program_id(0)

    @pl.when(i == 0)
    def _():
        kbuf[:, 0:tq, :] = jnp.zeros((2, tq, LANES), F32)
        vbuf[:, 0:tq, :] = jnp.zeros((2, tq, LANES), F32)

    for hp in range(2):
        cols = slice(hp * LANES, (hp + 1) * LANES)
        qs[hp] = q_ref[:, cols] * SCALE
        kbuf[hp, tq:2 * tq, :] = k_ref[:, cols]
        vbuf[hp, tq:2 * tq, :] = v_ref[:, cols]

    row = lax.broadcasted_iota(jnp.int32, (2 * blk, 2 * blk), 0) & (blk - 1)
    col = lax.broadcasted_iota(jnp.int32, (2 * blk, 2 * blk), 1)
    band = (col >= row) & (col <= row + (N_KEYS - 1))

    def problem(j, carry):
        b = j // d
        r = j % d
        qstart = b * span + r
        kstart = tq + (b - 1) * span + r
        cmin = jnp.where((i == 0) & (b == 0), blk, 0)
        mask = band & (col >= cmin)
        for hp in range(2):
            if d == 1:
                qidx = pl.ds(pl.multiple_of(qstart, blk), blk)
                kidx = pl.ds(pl.multiple_of(kstart, blk), 2 * blk)
            else:
                qidx = pl.ds(qstart, blk, stride=d)
                kidx = pl.ds(kstart, 2 * blk, stride=d)
            o2, lse2 = _pair_attention(qs[hp, qidx, :], kbuf[hp, kidx, :], vbuf[hp, kidx, :], mask)
            os_[hp, qidx, :] = o2
            ls_[hp, qidx, :] = lse2
        return carry

    lax.fori_loop(0, tq // blk, problem, 0, unroll=4)

    for hp in range(2):
        cols = slice(hp * LANES, (hp + 1) * LANES)
        o_ref[:, cols] = os_[hp]
        lse_ref[:, cols] = ls_[hp]
    kbuf[:, 0:tq, :] = kbuf[:, tq:2 * tq, :]
    vbuf[:, 0:tq, :] = vbuf[:, tq:2 * tq, :]


def _attn_prompt(qkv, g, d):
    rows = qkv.shape[0]
    tq = ATTN_TILE
    n_grp = len(ATTN_GROUPS)
    slab = pltpu.VMEM((2, tq, LANES), F32)
    slab2 = pltpu.VMEM((2, 2 * tq, LANES), F32)
    return pl.pallas_call(
        functools.partial(_attn_prompt_body, d),
        grid=(rows // tq,),
        in_specs=[
            pl.BlockSpec((tq, D_GROUP), lambda i: (i, g)),
            pl.BlockSpec((tq, D_GROUP), lambda i: (i, n_grp + g)),
            pl.BlockSpec((tq, D_GROUP), lambda i: (i, 2 * n_grp + g)),
        ],
        out_specs=[
            pl.BlockSpec((tq, D_GROUP), lambda i: (i, 0)),
            pl.BlockSpec((tq, D_GROUP), lambda i: (i, 0)),
        ],
        out_shape=[
            jax.ShapeDtypeStruct((rows, D_GROUP), F32),
            jax.ShapeDtypeStruct((rows, D_GROUP), F32),
        ],
        scratch_shapes=[slab, slab2, slab2, slab, slab],
        compiler_params=pltpu.CompilerParams(
            dimension_semantics=("arbitrary",), vmem_limit_bytes=VMEM_LIMIT),
        name=f"attn_prompt_d{d}",
    )(qkv, qkv, qkv)


def _mixer_sample_body(t_new, hist_lens, *refs):
    n_grp = len(ATTN_GROUPS)
    (u_ref, qkv_ref, hist_ref, w_ref, b_ref, lng_ref, lnb_ref) = refs[:7]
    cache_refs = refs[7:7 + 2 * n_grp]
    pos = 7 + 2 * n_grp
    c_ref, state_ref = refs[pos:pos + 2]
    ol_refs = refs[pos + 2:pos + 2 + 2 * n_grp]
    new_cache_refs = refs[pos + 2 + 2 * n_grp:pos + 2 + 4 * n_grp]
    xs = refs[pos + 2 + 4 * n_grp]

    n_hist = CONV_WIDTH - 1
    xs[0:n_hist, :] = hist_ref[...]
    xs[n_hist:n_hist + t_new, :] = u_ref[...]
    acc = jnp.broadcast_to(b_ref[...], (t_new, D_CONV))
    for k in range(CONV_WIDTH):
        acc = acc + w_ref[k:k + 1, :] * xs[k:k + t_new, :]
    c_ref[...] = _ln_swish(acc, lng_ref[...], lnb_ref[...])
    state_ref[...] = xs[t_new:t_new + n_hist, :]

    def window_mask(delta, d):
        return (delta >= 0) & ((delta & (d - 1)) == 0) & (delta <= (N_KEYS - 1) * d)

    pad_rows = jnp.zeros((LANES - t_new, D_GROUP), F32)
    t_new_idx = lax.broadcasted_iota(jnp.int32, (2 * t_new, LANES), 0) & (t_new - 1)
    j_new_idx = lax.broadcasted_iota(jnp.int32, (2 * t_new, LANES), 1)
    lane = lax.broadcasted_iota(jnp.int32, (D_GROUP, LANES), 1)
    for g, (_, d) in enumerate(ATTN_GROUPS):
        hl = hist_lens[g]
        kt_ref, vt_ref = cache_refs[2 * g], cache_refs[2 * g + 1]
        kn = jnp.concatenate([qkv_ref[:, D_ATTN + g * D_GROUP:D_ATTN + (g + 1) * D_GROUP], pad_rows], axis=0)
        vn = jnp.concatenate([qkv_ref[:, 2 * D_ATTN + g * D_GROUP:2 * D_ATTN + (g + 1) * D_GROUP], pad_rows], axis=0)

        t_idx = lax.broadcasted_iota(jnp.int32, (2 * t_new, hl), 0) & (t_new - 1)
        c_idx = lax.broadcasted_iota(jnp.int32, (2 * t_new, hl), 1)
        mask_c = window_mask(hl + t_idx - c_idx, d)
        mask_n = window_mask(t_new_idx - j_new_idx, d) & (j_new_idx < t_new)
        for hp in range(2):
            cols = slice(hp * LANES, (hp + 1) * LANES)
            q2 = qkv_ref[:, g * D_GROUP + hp * LANES:g * D_GROUP + (hp + 1) * LANES] * SCALE
            o2, lse2 = _pair_attention_cached(q2, kt_ref[cols, :], vt_ref[cols, :], kn[:, cols], vn[:, cols],
                                              mask_c, mask_n)
            ol_refs[2 * g][:, cols] = o2
            ol_refs[2 * g + 1][:, cols] = lse2

        for old_ref, new_ref, rows_new in ((kt_ref, new_cache_refs[2 * g], kn), (vt_ref, new_cache_refs[2 * g + 1], vn)):
            placed = pltpu.roll(rows_new.T, LANES - t_new, axis=1)
            shifted = pltpu.roll(old_ref[...], hl - t_new, axis=1)
            if hl > LANES:
                new_ref[:, 0:hl - LANES] = shifted[:, 0:hl - LANES]
            new_ref[:, hl - LANES:hl] = jnp.where(lane >= LANES - t_new, placed, shifted[:, hl - LANES:hl])


def _mixer_sample(layer, u, qkv, state_conv, caches, prev_new_caches, w, b, lng, lnb):
    n_grp = len(ATTN_GROUPS)
    depth, n_seq = caches[0].shape[:2]
    t_new = u.shape[0] // n_seq
    hist_lens = tuple(c.shape[3] for c in caches[0::2])
    n_hist = CONV_WIDTH - 1

    in_specs = [
        pl.BlockSpec((t_new, D_CONV), lambda n: (n, 0)),
        pl.BlockSpec((t_new, 3 * D_ATTN), lambda n: (n, 0)),
        pl.BlockSpec((None, None, n_hist, D_CONV), lambda n: (layer, n, 0, 0)),
        _layer_spec(w, layer),
        _layer_spec(b, layer),
        _layer_spec(lng, layer),
        _layer_spec(lnb, layer),
    ]
    for g in range(n_grp):
        for _ in range(2):
            in_specs.append(pl.BlockSpec((None, None, D_GROUP, hist_lens[g]), lambda n: (layer, n, 0, 0)))
    args = [u, qkv, state_conv, w, b, lng, lnb, *caches]
    aliases = {}
    n_fixed_out = 2 + 2 * n_grp
    if prev_new_caches is not None:
        for j, prev in enumerate(prev_new_caches):
            in_specs.append(pl.BlockSpec(memory_space=pl.ANY))
            aliases[len(args)] = n_fixed_out + j
            args.append(prev)

    out_specs = [
        pl.BlockSpec((t_new, D_CONV), lambda n: (n, 0)),
        pl.BlockSpec((None, n_hist, D_CONV), lambda n: (n, 0, 0)),
    ]
    out_shape = [
        jax.ShapeDtypeStruct((n_seq * t_new, D_CONV), F32),
        jax.ShapeDtypeStruct((n_seq, n_hist, D_CONV), F32),
    ]
    for g in range(n_grp):
        for _ in range(2):
            out_specs.append(pl.BlockSpec((t_new, D_GROUP), lambda n: (n, 0)))
            out_shape.append(jax.ShapeDtypeStruct((n_seq * t_new, D_GROUP), F32))
    for g in range(n_grp):
        for _ in range(2):
            out_specs.append(pl.BlockSpec((None, None, D_GROUP, hist_lens[g]), lambda n: (layer, n, 0, 0)))
            out_shape.append(jax.ShapeDtypeStruct((depth, n_seq, D_GROUP, hist_lens[g]), F32))

    scratch = [pltpu.VMEM((n_hist + t_new + 2, D_CONV), F32)]

    def body(*refs):
        if prev_new_caches is not None:
            k = len(prev_new_caches)
            n_in = 7 + 2 * n_grp
            refs = refs[:n_in] + refs[n_in + k:]
        _mixer_sample_body(t_new, hist_lens, *refs)

    outs = pl.pallas_call(
        body,
        grid=(n_seq,),
        in_specs=in_specs,
        out_specs=out_specs,
        out_shape=out_shape,
        scratch_shapes=scratch,
        input_output_aliases=aliases,
        compiler_params=pltpu.CompilerParams(
            dimension_semantics=("arbitrary",), vmem_limit_bytes=VMEM_LIMIT),
        name=f"mixer_sample_l{layer}",
    )(*args)
    c, state = outs[0], outs[1]
    ol = outs[2:2 + 2 * n_grp]
    new_caches = outs[2 + 2 * n_grp:]
    return c, state, ol, new_caches


FF_CHUNK = 512


def _post_body(tm, per_seq, final, conv_ln, *refs):
    n_grp = len(ATTN_GROUPS)
    x_ref, c_ref = refs[0:2]
    ol_refs = refs[2:2 + 2 * n_grp]
    pos = 2 + 2 * n_grp
    sg_ref = refs[pos]
    (wco_ref, wao_ref, wo_ref, g2_ref, wup_ref, fw_ref, fb_ref, wdn_ref) = refs[pos + 1:pos + 9]
    pos += 9
    if final:
        gf_ref = refs[pos]
        pos += 1
    if conv_ln:
        lng_ref, lnb_ref = refs[pos:pos + 2]
        pos += 2
    if per_seq:
        h0_ref, h1_ref = refs[pos:pos + 2]
        pos += 2
    y_ref, up_ref = refs[pos:pos + 2]
    carry = None if per_seq else refs[pos + 2]

    lses = [ol_refs[2 * g + 1][...] for g in range(n_grp)]
    lmax = functools.reduce(jnp.maximum, lses)
    es = [jnp.exp(l - lmax) for l in lses]
    den = functools.reduce(lambda a, b: a + b, es)
    attn = functools.reduce(lambda a, b: a + b, [(es[g] / den) * ol_refs[2 * g][...] for g in range(n_grp)])

    c = _ln_swish(c_ref[...], lng_ref[...], lnb_ref[...]) if conv_ln else c_ref[...]
    branch_conv = jnp.dot(c.astype(BF16), wco_ref[...], preferred_element_type=F32)
    branch_attn = jnp.dot(attn.astype(BF16), wao_ref[...], preferred_element_type=F32)
    mixed = sg_ref[:, 0:D_MODEL] * branch_conv + sg_ref[:, D_MODEL:2 * D_MODEL] * branch_attn
    x1 = x_ref[...] + jnp.dot(mixed.astype(BF16), wo_ref[...], preferred_element_type=F32)

    h = _rms(x1, g2_ref[...]).astype(BF16)
    if not per_seq:
        @pl.when(pl.program_id(0) == 0)
        def _():
            carry[...] = jnp.zeros(carry.shape, F32)
    out_rows = up_ref.shape[0]
    t_in_seq = lax.broadcasted_iota(jnp.int32, (tm, FF_CHUNK), 0) & (SUBLANES - 1)

    def conv3(c0):
        cols = slice(c0, c0 + FF_CHUNK)
        up = jnp.dot(h, wup_ref[:, cols], preferred_element_type=F32)
        up_ref[:, cols] = up[tm - out_rows:, :]
        head = up[0:SUBLANES, :] if per_seq else carry[:, cols]
        ext = jnp.concatenate([head, up], axis=0).reshape(tm // SUBLANES + 1, SUBLANES, FF_CHUNK)
        p1 = _shifted_rows(ext, SUBLANES - 1).reshape(tm, FF_CHUNK)
        p2 = _shifted_rows(ext, SUBLANES - 2).reshape(tm, FF_CHUNK)
        if per_seq:
            p1 = jnp.where(t_in_seq == 0, h1_ref[:, cols], p1)
            p2 = jnp.where(t_in_seq == 0, h0_ref[:, cols], jnp.where(t_in_seq == 1, h1_ref[:, cols], p2))
        else:
            carry[:, cols] = up[tm - SUBLANES:, :]
        return fw_ref[0:1, cols] * p2 + fw_ref[1:2, cols] * p1 + fw_ref[2:3, cols] * up + fb_ref[:, cols]

    acc = jnp.zeros((tm, D_MODEL), F32)
    for c0 in range(0, D_FF, FF_CHUNK):
        val = conv3(c0)
        gate = conv3(D_FF + c0)
        act = (jax.nn.silu(gate) * val).astype(BF16)
        acc = acc + jnp.dot(act, wdn_ref[c0:c0 + FF_CHUNK, :], preferred_element_type=F32)
    x2 = x1 + acc
    if final:
        x2 = _rms(x2, gf_ref[...])
    y_ref[...] = x2


def _post(layer, x, c, ol, sg, wco, wao, wo, g2, wup, fw, fb, wdn, gf, hist, conv_ln, tm):
    rows = x.shape[0]
    n_grp = len(ATTN_GROUPS)
    per_seq = hist is not None
    final = gf is not None
    row_spec = lambda w: pl.BlockSpec((tm, w), lambda i: (i, 0))
    in_specs = [row_spec(D_MODEL), row_spec(D_CONV)] + [row_spec(D_GROUP)] * (2 * n_grp) + [row_spec(2 * D_MODEL)]
    args = [x, c, *ol, sg]
    for a in (wco, wao, wo, g2, wup, fw, fb, wdn):
        in_specs.append(_layer_spec(a, layer))
        args.append(a)
    if final:
        in_specs.append(_const_spec(gf.shape))
        args.append(gf)
    if conv_ln is not None:
        for a in conv_ln:
            in_specs.append(_layer_spec(a, layer))
            args.append(a)
    if per_seq:
        in_specs += [row_spec(2 * D_FF)] * 2
        args += list(hist)
    up_rows = rows if per_seq else SUBLANES
    up_spec = row_spec(2 * D_FF) if per_seq else pl.BlockSpec((SUBLANES, 2 * D_FF), lambda i: (0, 0))
    scratch = [] if per_seq else [pltpu.VMEM((SUBLANES, 2 * D_FF), F32)]
    return pl.pallas_call(
        functools.partial(_post_body, tm, per_seq, final, conv_ln is not None),
        grid=(rows // tm,),
        in_specs=in_specs,
        out_specs=[row_spec(D_MODEL), up_spec],
        out_shape=[
            jax.ShapeDtypeStruct((rows, D_MODEL), F32),
            jax.ShapeDtypeStruct((up_rows, 2 * D_FF), F32),
        ],
        scratch_shapes=scratch,
        compiler_params=pltpu.CompilerParams(
            dimension_semantics=("arbitrary",), vmem_limit_bytes=VMEM_LIMIT),
        name="post_sample" if per_seq else "post_prompt",
    )(*args)


def kernel(x_prompt, x_sample, state_conv, cache_k_w128, cache_v_w128, cache_k_w512, cache_v_w512, cache_k_w2048, cache_v_w2048, state_ffn_conv, norm_attn_g, w_in, conv_dw_w, conv_dw_b, conv_ln_g, conv_ln_b, w_conv_out, w_attn_out, w_out, norm_ffn_g, w_up, ffn_dw_w, ffn_dw_b, w_down, norm_final_g):
    depth = w_in.shape[0]
    n_b, seq, _ = x_prompt.shape
    n_s, t_new, _ = x_sample.shape
    assert n_b == 1 and seq % ATTN_TILE == 0 and seq % ROW_TILE == 0 and ROW_TILE % CONV_ROW_CHUNK == 0
    assert t_new == SUBLANES
    n_grp = len(ATTN_GROUPS)
    caches = [jnp.transpose(c, (0, 1, 3, 4, 2)).reshape(c.shape[:2] + (D_GROUP, c.shape[2])) for c in
              (cache_k_w128, cache_v_w128, cache_k_w512, cache_v_w512, cache_k_w2048, cache_v_w2048)]

    xp = x_prompt.reshape(seq, D_MODEL)
    xs = x_sample.reshape(n_s * t_new, D_MODEL)
    rows3 = lambda a: a.reshape(depth, 1, -1)
    gf = norm_final_g.reshape(1, -1)
    w_in_b = w_in.astype(BF16)
    wco, wao, wo = w_conv_out.astype(BF16), w_attn_out.astype(BF16), w_out.astype(BF16)
    wup, wdn = w_up.astype(BF16), w_down.astype(BF16)
    g1, g2 = rows3(norm_attn_g), rows3(norm_ffn_g)
    cw, cb, lng, lnb = conv_dw_w, rows3(conv_dw_b), rows3(conv_ln_g), rows3(conv_ln_b)
    cw8 = jnp.broadcast_to(cw[:, :, None, :], (depth, CONV_WIDTH, SUBLANES, D_CONV))
    fw, fb = ffn_dw_w, rows3(ffn_dw_b)

    conv_p, conv_s, ffn_p, ffn_s = [], [], [], []
    kv_p = [[[], []] for _ in range(n_grp)]
    new_caches = None
    for l in range(depth):
        last = l == depth - 1
        post_w = (wco, wao, wo, g2, wup, fw, fb, wdn, gf if last else None)

        u, qkv, sg = _inproj(l, xp, g1, w_in_b, ROW_TILE)
        y = _conv_prompt(l, u, cw8, cb, ROW_TILE)
        ol = []
        for g, (_, d) in enumerate(ATTN_GROUPS):
            ol += _attn_prompt(qkv, g, d)
        xp, up_tail = _post(l, xp, y, ol, sg, *post_w, None, (lng, lnb), ROW_TILE)
        conv_p.append(u[seq - (CONV_WIDTH - 1):].reshape(1, CONV_WIDTH - 1, D_CONV))
        ffn_p.append(up_tail[SUBLANES - (FFN_CONV_WIDTH - 1):].reshape(1, FFN_CONV_WIDTH - 1, 2 * D_FF))
        for g, (window, _) in enumerate(ATTN_GROUPS):
            keep = min(window, seq)
            for j in range(2):
                c0 = (1 + j) * D_ATTN + g * D_GROUP
                kv_p[g][j].append(qkv[seq - keep:, c0:c0 + D_GROUP].reshape(1, keep, HEADS_PER_GROUP, HEAD_DIM))

        us, qkvs, sgs = _inproj(l, xs, g1, w_in_b, n_s * t_new)
        cs, state_s, ols, new_caches = _mixer_sample(l, us, qkvs, state_conv, caches, new_caches, cw, cb, lng, lnb)
        hist = tuple(jnp.repeat(state_ffn_conv[l][:, j], t_new, axis=0) for j in range(FFN_CONV_WIDTH - 1))
        xs, up_s = _post(l, xs, cs, ols, sgs, *post_w, hist, None, n_s * t_new)
        conv_s.append(state_s)
        ffn_s.append(up_s.reshape(n_s, t_new, 2 * D_FF)[:, t_new - (FFN_CONV_WIDTH - 1):])

    st = lambda lst: jnp.stack(lst, axis=0)
    kv_s = [jnp.transpose(nc.reshape(nc.shape[:2] + (HEADS_PER_GROUP, HEAD_DIM, nc.shape[3])), (0, 1, 4, 2, 3))
            for nc in new_caches]
    return (xp.reshape(n_b, seq, D_MODEL), xs.reshape(n_s, t_new, D_MODEL),
            st(conv_p), st(conv_s),
            st(kv_p[0][0]), st(kv_p[0][1]), kv_s[0], kv_s[1],
            st(kv_p[1][0]), st(kv_p[1][1]), kv_s[2], kv_s[3],
            st(kv_p[2][0]), st(kv_p[2][1]), kv_s[4], kv_s[5],
            st(ffn_p), st(ffn_s))
```

```python
import functools

import jax
import jax.numpy as jnp
from jax import lax
from jax.experimental import pallas as pl
from jax.experimental.pallas import tpu as pltpu

F32 = jnp.float32
BF16 = jnp.bfloat16

D_MODEL = 1024
D_CONV = 1024
CONV_WIDTH = 31
HEAD_DIM = 64
HEADS_PER_GROUP = 4
ATTN_GROUPS = ((128, 1), (512, 4), (2048, 16))
N_KEYS = 129
D_GROUP = HEADS_PER_GROUP * HEAD_DIM
D_ATTN = len(ATTN_GROUPS) * D_GROUP
D_FF = 3 * D_MODEL
FFN_CONV_WIDTH = 3
EPS = 1e-6
SCALE = HEAD_DIM ** -0.5

LANES = 128
SUBLANES = 8
VMEM_LIMIT = 60 * 1024 * 1024

ROW_TILE = 512
CONV_HALO = 32
CONV_ROW_CHUNK = 256
ATTN_TILE = 2048
ATTN_BLOCK = 128


def _const_spec(shape):
    nd = len(shape)
    return pl.BlockSpec(shape, lambda *_: (0,) * nd, pipeline_mode=pl.Buffered(1))


def _layer_spec(stacked, layer):
    rest = stacked.shape[1:]
    return pl.BlockSpec((None,) + rest, lambda *_: (layer,) + (0,) * len(rest), pipeline_mode=pl.Buffered(1))


def _rms(x, g):
    return x * lax.rsqrt(jnp.mean(x * x, axis=-1, keepdims=True) + EPS) * g


PROJ_CHUNK = 512


def _glu_project(h, w_ref, u_ref, row0):
    rows = h.shape[0]
    for j in range(0, D_CONV, PROJ_CHUNK):
        a = jnp.dot(h, w_ref[:, j:j + PROJ_CHUNK], preferred_element_type=F32)
        b = jnp.dot(h, w_ref[:, D_CONV + j:D_CONV + j + PROJ_CHUNK], preferred_element_type=F32)
        u_ref[row0:row0 + rows, j:j + PROJ_CHUNK] = a * jax.nn.sigmoid(b)


def _qkv_gate_tasks(h, w_ref, qkv_ref, sg_ref):
    tasks = []
    q0 = 2 * D_CONV
    g0 = q0 + 3 * D_ATTN

    def qkv_task(j, width):
        def run():
            qkv_ref[:, j:j + width] = jnp.dot(h, w_ref[:, q0 + j:q0 + j + width], preferred_element_type=F32)
        return run

    def gate_task(j):
        def run():
            sg_ref[:, j:j + PROJ_CHUNK] = jax.nn.sigmoid(
                jnp.dot(h, w_ref[:, g0 + j:g0 + j + PROJ_CHUNK], preferred_element_type=F32))
        return run

    for j in range(0, 3 * D_ATTN, PROJ_CHUNK):
        tasks.append(qkv_task(j, min(PROJ_CHUNK, 3 * D_ATTN - j)))
    for j in range(0, 2 * D_MODEL, PROJ_CHUNK):
        tasks.append(gate_task(j))
    return tasks


def _inproj_body(x_ref, g_ref, w_ref, u_ref, qkv_ref, sg_ref):
    h = _rms(x_ref[...], g_ref[...]).astype(BF16)
    _glu_project(h, w_ref, u_ref, 0)
    for task in _qkv_gate_tasks(h, w_ref, qkv_ref, sg_ref):
        task()


def _inproj(layer, x, g, w_in, tm):
    rows = x.shape[0]
    return pl.pallas_call(
        _inproj_body,
        grid=(rows // tm,),
        in_specs=[
            pl.BlockSpec((tm, D_MODEL), lambda i: (i, 0)),
            _layer_spec(g, layer),
            _layer_spec(w_in, layer),
        ],
        out_specs=[
            pl.BlockSpec((tm, D_CONV), lambda i: (i, 0)),
            pl.BlockSpec((tm, 3 * D_ATTN), lambda i: (i, 0)),
            pl.BlockSpec((tm, 2 * D_MODEL), lambda i: (i, 0)),
        ],
        out_shape=[
            jax.ShapeDtypeStruct((rows, D_CONV), F32),
            jax.ShapeDtypeStruct((rows, 3 * D_ATTN), F32),
            jax.ShapeDtypeStruct((rows, 2 * D_MODEL), F32),
        ],
        compiler_params=pltpu.CompilerParams(
            dimension_semantics=("parallel",), vmem_limit_bytes=VMEM_LIMIT),
        name="inproj",
    )(x, g, w_in)


def _ln_swish(y, g, b):
    mu = jnp.mean(y, axis=-1, keepdims=True)
    yc = y - mu
    var = jnp.mean(yc * yc, axis=-1, keepdims=True)
    z = yc * lax.rsqrt(var + EPS) * g + b
    return z * jax.nn.sigmoid(z)


def _shifted_rows(blk, s):
    if s == 0:
        return blk[:-1]
    rot = pltpu.roll(blk, SUBLANES - s, axis=1)
    sub = lax.broadcasted_iota(jnp.int32, rot[:-1].shape, 1)
    return jnp.where(sub < SUBLANES - s, rot[:-1], rot[1:])


def _conv31_lanes(buf, w8_ref, b_ref, ybuf, rows, l0):
    first = CONV_HALO - (CONV_WIDTH - 1)
    rc = CONV_ROW_CHUNK
    nv = rc // SUBLANES
    lanes = slice(l0, l0 + LANES)

    def row_chunk(i, carry):
        r0 = pl.multiple_of(i * rc, rc)
        blk = buf[pl.ds(r0, rc + CONV_HALO + SUBLANES), lanes].reshape(nv + 5, SUBLANES, LANES)
        acc = jnp.broadcast_to(b_ref[:, lanes], (nv, SUBLANES, LANES))
        for s in range(SUBLANES):
            sh = _shifted_rows(blk, s)
            for a in range(5):
                k = SUBLANES * a + s - first
                if 0 <= k < CONV_WIDTH:
                    acc = acc + w8_ref[k, :, lanes] * sh[a:a + nv]
        ybuf[pl.ds(r0, rc), lanes] = acc.reshape(rc, LANES)
        return carry

    lax.fori_loop(0, rows // rc, row_chunk, 0)


def _conv_prompt_body(tm, u_ref, cw8_ref, cb_ref, y_ref, ubuf):
    @pl.when(pl.program_id(0) == 0)
    def _():
        ubuf[0:CONV_HALO, :] = jnp.zeros((CONV_HALO, D_CONV), F32)
        ubuf[CONV_HALO + tm:, :] = jnp.zeros((SUBLANES, D_CONV), F32)

    ubuf[CONV_HALO:CONV_HALO + tm, :] = u_ref[...]
    for l0 in range(0, D_CONV, LANES):
        _conv31_lanes(ubuf, cw8_ref, cb_ref, y_ref, tm, l0)
    ubuf[0:CONV_HALO, :] = ubuf[tm:tm + CONV_HALO, :]


def _conv_prompt(layer, u, cw8, cb, tm):
    rows = u.shape[0]
    return pl.pallas_call(
        functools.partial(_conv_prompt_body, tm),
        grid=(rows // tm,),
        in_specs=[
            pl.BlockSpec((tm, D_CONV), lambda i: (i, 0)),
            _layer_spec(cw8, layer),
            _layer_spec(cb, layer),
        ],
        out_specs=pl.BlockSpec((tm, D_CONV), lambda i: (i, 0)),
        out_shape=jax.ShapeDtypeStruct((rows, D_CONV), F32),
        scratch_shapes=[pltpu.VMEM((CONV_HALO + tm + SUBLANES, D_CONV), F32)],
        compiler_params=pltpu.CompilerParams(
            dimension_semantics=("arbitrary",), vmem_limit_bytes=VMEM_LIMIT),
        name="conv_prompt",
    )(u, cw8, cb)


def _pair_attention(q2, k2, v2, mask):
    m_rows = q2.shape[0]
    lane = lax.broadcasted_iota(jnp.int32, q2.shape, 1)
    zero = jnp.zeros_like(q2)
    qq = jnp.concatenate([jnp.where(lane < HEAD_DIM, q2, zero), jnp.where(lane >= HEAD_DIM, q2, zero)], axis=0)
    s = lax.dot_general(qq.astype(BF16), k2.astype(BF16), (((1,), (1,)), ((), ())), preferred_element_type=F32)
    s = jnp.where(mask, s, -jnp.inf)
    mx = jnp.max(s, axis=-1, keepdims=True)
    p = jnp.exp(s - mx)
    l = jnp.sum(p, axis=-1, keepdims=True)
    pv = jnp.dot(p.astype(BF16), v2.astype(BF16), preferred_element_type=F32)
    o = pv / l
    lse = jnp.broadcast_to(mx + jnp.log(l), o.shape)
    first = lane < HEAD_DIM
    return (jnp.where(first, o[:m_rows], o[m_rows:]),
            jnp.where(first, lse[:m_rows], lse[m_rows:]))


def _pair_attention_cached(q2, kt, vt, kn, vn, mask_c, mask_n):
    m_rows = q2.shape[0]
    lane = lax.broadcasted_iota(jnp.int32, q2.shape, 1)
    zero = jnp.zeros_like(q2)
    qq = jnp.concatenate([jnp.where(lane < HEAD_DIM, q2, zero), jnp.where(lane >= HEAD_DIM, q2, zero)],
                         axis=0).astype(BF16)
    nt = (((1,), (1,)), ((), ()))
    s_c = jnp.where(mask_c, jnp.dot(qq, kt.astype(BF16), preferred_element_type=F32), -jnp.inf)
    s_n = jnp.where(mask_n, lax.dot_general(qq, kn.astype(BF16), nt, preferred_element_type=F32), -jnp.inf)
    mx = jnp.maximum(jnp.max(s_c, axis=-1, keepdims=True), jnp.max(s_n, axis=-1, keepdims=True))
    p_c = jnp.exp(s_c - mx)
    p_n = jnp.exp(s_n - mx)
    l = jnp.sum(p_c, axis=-1, keepdims=True) + jnp.sum(p_n, axis=-1, keepdims=True)
    pv = (lax.dot_general(p_c.astype(BF16), vt.astype(BF16), nt, preferred_element_type=F32)
          + jnp.dot(p_n.astype(BF16), vn.astype(BF16), preferred_element_type=F32))
    o = pv / l
    lse = jnp.broadcast_to(mx + jnp.log(l), o.shape)
    first = lane < HEAD_DIM
    return (jnp.where(first, o[:m_rows], o[m_rows:]),
            jnp.where(first, lse[:m_rows], lse[m_rows:]))


def _attn_prompt_body(d, q_ref, k_ref, v_ref, o_ref, lse_ref, qs, kbuf, vbuf, os_, ls_):
    tq = ATTN_TILE
    blk = ATTN_BLOCK
    span = blk * d
    i = pl.program_id(0)

    @pl.when(i == 0)
    def _():
        kbuf[:, tq - span:tq, :] = jnp.zeros((2, span, LANES), F32)
        vbuf[:, tq - span:tq, :] = jnp.zeros((2, span, LANES), F32)

    for hp in range(2):
        cols = slice(hp * LANES, (hp + 1) * LANES)
        qs[hp] = q_ref[:, cols] * SCALE
        kbuf[hp, tq:2 * tq, :] = k_ref[:, cols]
        vbuf[hp, tq:2 * tq, :] = v_ref[:, cols]

    row = lax.broadcasted_iota(jnp.int32, (2 * blk, 2 * blk), 0) & (blk - 1)
    col = lax.broadcasted_iota(jnp.int32, (2 * blk, 2 * blk), 1)
    band = (col >= row) & (col <= row + (N_KEYS - 1))

    def problem(j, carry):
        b = j // d
        r = j % d
        qstart = b * span + r
        kstart = tq + (b - 1) * span + r
        cmin = jnp.where((i == 0) & (b == 0), blk, 0)
        mask = band & (col >= cmin)
        for hp in range(2):
            if d == 1:
                qidx = pl.ds(pl.multiple_of(qstart, blk), blk)
                kidx = pl.ds(pl.multiple_of(kstart, blk), 2 * blk)
            else:
                qidx = pl.ds(qstart, blk, stride=d)
                kidx = pl.ds(kstart, 2 * blk, stride=d)
            o2, lse2 = _pair_attention(qs[hp, qidx, :], kbuf[hp, kidx, :], vbuf[hp, kidx, :], mask)
            os_[hp, qidx, :] = o2
            ls_[hp, qidx, :] = lse2
        return carry

    lax.fori_loop(0, tq // blk, problem, 0, unroll=4)

    for hp in range(2):
        cols = slice(hp * LANES, (hp + 1) * LANES)
        o_ref[:, cols] = os_[hp]
        lse_ref[:, cols] = ls_[hp]
    kbuf[:, tq - span:tq, :] = kbuf[:, 2 * tq - span:2 * tq, :]
    vbuf[:, tq - span:tq, :] = vbuf[:, 2 * tq - span:2 * tq, :]


def _attn_prompt(qkv, g, d):
    rows = qkv.shape[0]
    tq = ATTN_TILE
    n_grp = len(ATTN_GROUPS)
    slab = pltpu.VMEM((2, tq, LANES), F32)
    slab2 = pltpu.VMEM((2, 2 * tq, LANES), F32)
    return pl.pallas_call(
        functools.partial(_attn_prompt_body, d),
        grid=(rows // tq,),
        in_specs=[
            pl.BlockSpec((tq, D_GROUP), lambda i: (i, g)),
            pl.BlockSpec((tq, D_GROUP), lambda i: (i, n_grp + g)),
            pl.BlockSpec((tq, D_GROUP), lambda i: (i, 2 * n_grp + g)),
        ],
        out_specs=[
            pl.BlockSpec((tq, D_GROUP), lambda i: (i, 0)),
            pl.BlockSpec((tq, D_GROUP), lambda i: (i, 0)),
        ],
        out_shape=[
            jax.ShapeDtypeStruct((rows, D_GROUP), F32),
            jax.ShapeDtypeStruct((rows, D_GROUP), F32),
        ],
        scratch_shapes=[slab, slab2, slab2, slab, slab],
        compiler_params=pltpu.CompilerParams(
            dimension_semantics=("arbitrary",), vmem_limit_bytes=VMEM_LIMIT),
        name=f"attn_prompt_d{d}",
    )(qkv, qkv, qkv)


def _mixer_sample_body(t_new, hist_lens, *refs):
    n_grp = len(ATTN_GROUPS)
    (u_ref, qkv_ref, hist_ref, w_ref, b_ref, lng_ref, lnb_ref) = refs[:7]
    cache_refs = refs[7:7 + 2 * n_grp]
    pos = 7 + 2 * n_grp
    c_ref, state_ref = refs[pos:pos + 2]
    ol_refs = refs[pos + 2:pos + 2 + 2 * n_grp]
    new_cache_refs = refs[pos + 2 + 2 * n_grp:pos + 2 + 4 * n_grp]
    xs = refs[pos + 2 + 4 * n_grp]

    n_hist = CONV_WIDTH - 1
    xs[0:n_hist, :] = hist_ref[...]
    xs[n_hist:n_hist + t_new, :] = u_ref[...]
    acc = jnp.broadcast_to(b_ref[...], (t_new, D_CONV))
    for k in range(CONV_WIDTH):
        acc = acc + w_ref[k:k + 1, :] * xs[k:k + t_new, :]
    c_ref[...] = _ln_swish(acc, lng_ref[...], lnb_ref[...])
    state_ref[...] = xs[t_new:t_new + n_hist, :]

    def window_mask(delta, d):
        return (delta >= 0) & ((delta & (d - 1)) == 0) & (delta <= (N_KEYS - 1) * d)

    pad_rows = jnp.zeros((LANES - t_new, D_GROUP), F32)
    t_new_idx = lax.broadcasted_iota(jnp.int32, (2 * t_new, LANES), 0) & (t_new - 1)
    j_new_idx = lax.broadcasted_iota(jnp.int32, (2 * t_new, LANES), 1)
    lane = lax.broadcasted_iota(jnp.int32, (D_GROUP, LANES), 1)
    for g, (_, d) in enumerate(ATTN_GROUPS):
        hl = hist_lens[g]
        kt_ref, vt_ref = cache_refs[2 * g], cache_refs[2 * g + 1]
        kn = jnp.concatenate([qkv_ref[:, D_ATTN + g * D_GROUP:D_ATTN + (g + 1) * D_GROUP], pad_rows], axis=0)
        vn = jnp.concatenate([qkv_ref[:, 2 * D_ATTN + g * D_GROUP:2 * D_ATTN + (g + 1) * D_GROUP], pad_rows], axis=0)

        t_idx = lax.broadcasted_iota(jnp.int32, (2 * t_new, hl), 0) & (t_new - 1)
        c_idx = lax.broadcasted_iota(jnp.int32, (2 * t_new, hl), 1)
        mask_c = window_mask(hl + t_idx - c_idx, d)
        mask_n = window_mask(t_new_idx - j_new_idx, d) & (j_new_idx < t_new)
        for hp in range(2):
            cols = slice(hp * LANES, (hp + 1) * LANES)
            q2 = qkv_ref[:, g * D_GROUP + hp * LANES:g * D_GROUP + (hp + 1) * LANES] * SCALE
            o2, lse2 = _pair_attention_cached(q2, kt_ref[cols, :], vt_ref[cols, :], kn[:, cols], vn[:, cols],
                                              mask_c, mask_n)
            ol_refs[2 * g][:, cols] = o2
            ol_refs[2 * g + 1][:, cols] = lse2

        for old_ref, new_ref, rows_new in ((kt_ref, new_cache_refs[2 * g], kn), (vt_ref, new_cache_refs[2 * g + 1], vn)):
            placed = pltpu.roll(rows_new.T, LANES - t_new, axis=1)
            shifted = pltpu.roll(old_ref[...], hl - t_new, axis=1)
            if hl > LANES:
                new_ref[:, 0:hl - LANES] = shifted[:, 0:hl - LANES]
            new_ref[:, hl - LANES:hl] = jnp.where(lane >= LANES - t_new, placed, shifted[:, hl - LANES:hl])


def _mixer_sample(layer, u, qkv, state_conv, caches, prev_new_caches, w, b, lng, lnb):
    n_grp = len(ATTN_GROUPS)
    depth, n_seq = caches[0].shape[:2]
    t_new = u.shape[0] // n_seq
    hist_lens = tuple(c.shape[3] for c in caches[0::2])
    n_hist = CONV_WIDTH - 1

    in_specs = [
        pl.BlockSpec((t_new, D_CONV), lambda n: (n, 0)),
        pl.BlockSpec((t_new, 3 * D_ATTN), lambda n: (n, 0)),
        pl.BlockSpec((None, None, n_hist, D_CONV), lambda n: (layer, n, 0, 0)),
        _layer_spec(w, layer),
        _layer_spec(b, layer),
        _layer_spec(lng, layer),
        _layer_spec(lnb, layer),
    ]
    for g in range(n_grp):
        for _ in range(2):
            in_specs.append(pl.BlockSpec((None, None, D_GROUP, hist_lens[g]), lambda n: (layer, n, 0, 0)))
    args = [u, qkv, state_conv, w, b, lng, lnb, *caches]
    aliases = {}
    n_fixed_out = 2 + 2 * n_grp
    if prev_new_caches is not None:
        for j, prev in enumerate(prev_new_caches):
            in_specs.append(pl.BlockSpec(memory_space=pl.ANY))
            aliases[len(args)] = n_fixed_out + j
            args.append(prev)

    out_specs = [
        pl.BlockSpec((t_new, D_CONV), lambda n: (n, 0)),
        pl.BlockSpec((None, n_hist, D_CONV), lambda n: (n, 0, 0)),
    ]
    out_shape = [
        jax.ShapeDtypeStruct((n_seq * t_new, D_CONV), F32),
        jax.ShapeDtypeStruct((n_seq, n_hist, D_CONV), F32),
    ]
    for g in range(n_grp):
        for _ in range(2):
            out_specs.append(pl.BlockSpec((t_new, D_GROUP), lambda n: (n, 0)))
            out_shape.append(jax.ShapeDtypeStruct((n_seq * t_new, D_GROUP), F32))
    for g in range(n_grp):
        for _ in range(2):
            out_specs.append(pl.BlockSpec((None, None, D_GROUP, hist_lens[g]), lambda n: (layer, n, 0, 0)))
            out_shape.append(jax.ShapeDtypeStruct((depth, n_seq, D_GROUP, hist_lens[g]), F32))

    scratch = [pltpu.VMEM((n_hist + t_new + 2, D_CONV), F32)]

    def body(*refs):
        if prev_new_caches is not None:
            k = len(prev_new_caches)
            n_in = 7 + 2 * n_grp
            refs = refs[:n_in] + refs[n_in + k:]
        _mixer_sample_body(t_new, hist_lens, *refs)

    outs = pl.pallas_call(
        body,
        grid=(n_seq,),
        in_specs=in_specs,
        out_specs=out_specs,
        out_shape=out_shape,
        scratch_shapes=scratch,
        input_output_aliases=aliases,
        compiler_params=pltpu.CompilerParams(
            dimension_semantics=("arbitrary",), vmem_limit_bytes=VMEM_LIMIT),
        name=f"mixer_sample_l{layer}",
    )(*args)
    c, state = outs[0], outs[1]
    ol = outs[2:2 + 2 * n_grp]
    new_caches = outs[2 + 2 * n_grp:]
    return c, state, ol, new_caches


FF_CHUNK = 512


def _post_body(tm, per_seq, final, conv_ln, *refs):
    n_grp = len(ATTN_GROUPS)
    x_ref, c_ref = refs[0:2]
    ol_refs = refs[2:2 + 2 * n_grp]
    pos = 2 + 2 * n_grp
    sg_ref = refs[pos]
    (wco_ref, wao_ref, wo_ref, g2_ref, wup_ref, fw_ref, fb_ref, wdn_ref) = refs[pos + 1:pos + 9]
    pos += 9
    if final:
        gf_ref = refs[pos]
        pos += 1
    if conv_ln:
        lng_ref, lnb_ref = refs[pos:pos + 2]
        pos += 2
    if per_seq:
        h0_ref, h1_ref = refs[pos:pos + 2]
        pos += 2
    y_ref, up_ref = refs[pos:pos + 2]
    carry = None if per_seq else refs[pos + 2]

    lses = [ol_refs[2 * g + 1][...] for g in range(n_grp)]
    lmax = functools.reduce(jnp.maximum, lses)
    es = [jnp.exp(l - lmax) for l in lses]
    den = functools.reduce(lambda a, b: a + b, es)
    attn = functools.reduce(lambda a, b: a + b, [(es[g] / den) * ol_refs[2 * g][...] for g in range(n_grp)])

    c = _ln_swish(c_ref[...], lng_ref[...], lnb_ref[...]) if conv_ln else c_ref[...]
    branch_conv = jnp.dot(c.astype(BF16), wco_ref[...], preferred_element_type=F32)
    branch_attn = jnp.dot(attn.astype(BF16), wao_ref[...], preferred_element_type=F32)
    mixed = sg_ref[:, 0:D_MODEL] * branch_conv + sg_ref[:, D_MODEL:2 * D_MODEL] * branch_attn
    x1 = x_ref[...] + jnp.dot(mixed.astype(BF16), wo_ref[...], preferred_element_type=F32)

    h = _rms(x1, g2_ref[...]).astype(BF16)
    if not per_seq:
        @pl.when(pl.program_id(0) == 0)
        def _():
            carry[...] = jnp.zeros(carry.shape, F32)
    out_rows = up_ref.shape[0]
    t_in_seq = lax.broadcasted_iota(jnp.int32, (tm, FF_CHUNK), 0) & (SUBLANES - 1)

    def conv3(c0):
        cols = slice(c0, c0 + FF_CHUNK)
        up = jnp.dot(h, wup_ref[:, cols], preferred_element_type=F32)
        up_ref[:, cols] = up[tm - out_rows:, :]
        head = up[0:SUBLANES, :] if per_seq else carry[:, cols]
        ext = jnp.concatenate([head, up], axis=0).reshape(tm // SUBLANES + 1, SUBLANES, FF_CHUNK)
        p1 = _shifted_rows(ext, SUBLANES - 1).reshape(tm, FF_CHUNK)
        p2 = _shifted_rows(ext, SUBLANES - 2).reshape(tm, FF_CHUNK)
        if per_seq:
            p1 = jnp.where(t_in_seq == 0, h1_ref[:, cols], p1)
            p2 = jnp.where(t_in_seq == 0, h0_ref[:, cols], jnp.where(t_in_seq == 1, h1_ref[:, cols], p2))
        else:
            carry[:, cols] = up[tm - SUBLANES:, :]
        return fw_ref[0:1, cols] * p2 + fw_ref[1:2, cols] * p1 + fw_ref[2:3, cols] * up + fb_ref[:, cols]

    acc = jnp.zeros((tm, D_MODEL), F32)
    for c0 in range(0, D_FF, FF_CHUNK):
        val = conv3(c0)
        gate = conv3(D_FF + c0)
        act = (jax.nn.silu(gate) * val).astype(BF16)
        acc = acc + jnp.dot(act, wdn_ref[c0:c0 + FF_CHUNK, :], preferred_element_type=F32)
    x2 = x1 + acc
    if final:
        x2 = _rms(x2, gf_ref[...])
    y_ref[...] = x2


def _post(layer, x, c, ol, sg, wco, wao, wo, g2, wup, fw, fb, wdn, gf, hist, conv_ln, tm):
    rows = x.shape[0]
    n_grp = len(ATTN_GROUPS)
    per_seq = hist is not None
    final = gf is not None
    row_spec = lambda w: pl.BlockSpec((tm, w), lambda i: (i, 0))
    in_specs = [row_spec(D_MODEL), row_spec(D_CONV)] + [row_spec(D_GROUP)] * (2 * n_grp) + [row_spec(2 * D_MODEL)]
    args = [x, c, *ol, sg]
    for a in (wco, wao, wo, g2, wup, fw, fb, wdn):
        in_specs.append(_layer_spec(a, layer))
        args.append(a)
    if final:
        in_specs.append(_const_spec(gf.shape))
        args.append(gf)
    if conv_ln is not None:
        for a in conv_ln:
            in_specs.append(_layer_spec(a, layer))
            args.append(a)
    if per_seq:
        in_specs += [row_spec(2 * D_FF)] * 2
        args += list(hist)
    up_rows = rows if per_seq else SUBLANES
    up_spec = row_spec(2 * D_FF) if per_seq else pl.BlockSpec((SUBLANES, 2 * D_FF), lambda i: (0, 0))
    scratch = [] if per_seq else [pltpu.VMEM((SUBLANES, 2 * D_FF), F32)]
    return pl.pallas_call(
        functools.partial(_post_body, tm, per_seq, final, conv_ln is not None),
        grid=(rows // tm,),
        in_specs=in_specs,
        out_specs=[row_spec(D_MODEL), up_spec],
        out_shape=[
            jax.ShapeDtypeStruct((rows, D_MODEL), F32),
            jax.ShapeDtypeStruct((up_rows, 2 * D_FF), F32),
        ],
        scratch_shapes=scratch,
        compiler_params=pltpu.CompilerParams(
            dimension_semantics=("arbitrary",), vmem_limit_bytes=VMEM_LIMIT),
        name="post_sample" if per_seq else "post_prompt",
    )(*args)


def kernel(x_prompt, x_sample, state_conv, cache_k_w128, cache_v_w128, cache_k_w512, cache_v_w512, cache_k_w2048, cache_v_w2048, state_ffn_conv, norm_attn_g, w_in, conv_dw_w, conv_dw_b, conv_ln_g, conv_ln_b, w_conv_out, w_attn_out, w_out, norm_ffn_g, w_up, ffn_dw_w, ffn_dw_b, w_down, norm_final_g):
    depth = w_in.shape[0]
    n_b, seq, _ = x_prompt.shape
    n_s, t_new, _ = x_sample.shape
    assert n_b == 1 and seq % ATTN_TILE == 0 and seq % ROW_TILE == 0 and ROW_TILE % CONV_ROW_CHUNK == 0
    assert t_new == SUBLANES
    n_grp = len(ATTN_GROUPS)
    caches = [jnp.transpose(c, (0, 1, 3, 4, 2)).reshape(c.shape[:2] + (D_GROUP, c.shape[2])) for c in
              (cache_k_w128, cache_v_w128, cache_k_w512, cache_v_w512, cache_k_w2048, cache_v_w2048)]

    xp = x_prompt.reshape(seq, D_MODEL)
    xs = x_sample.reshape(n_s * t_new, D_MODEL)
    rows3 = lambda a: a.reshape(depth, 1, -1)
    gf = norm_final_g.reshape(1, -1)
    w_in_b = w_in.astype(BF16)
    wco, wao, wo = w_conv_out.astype(BF16), w_attn_out.astype(BF16), w_out.astype(BF16)
    wup, wdn = w_up.astype(BF16), w_down.astype(BF16)
    g1, g2 = rows3(norm_attn_g), rows3(norm_ffn_g)
    cw, cb, lng, lnb = conv_dw_w, rows3(conv_dw_b), rows3(conv_ln_g), rows3(conv_ln_b)
    cw8 = jnp.broadcast_to(cw[:, :, None, :], (depth, CONV_WIDTH, SUBLANES, D_CONV))
    fw, fb = ffn_dw_w, rows3(ffn_dw_b)

    conv_p, conv_s, ffn_p, ffn_s = [], [], [], []
    kv_p = [[[], []] for _ in range(n_grp)]
    new_caches = None
    for l in range(depth):
        last = l == depth - 1
        post_w = (wco, wao, wo, g2, wup, fw, fb, wdn, gf if last else None)

        u, qkv, sg = _inproj(l, xp, g1, w_in_b, ROW_TILE)
        y = _conv_prompt(l, u, cw8, cb, ROW_TILE)
        ol = []
        for g, (_, d) in enumerate(ATTN_GROUPS):
            ol += _attn_prompt(qkv, g, d)
        xp, up_tail = _post(l, xp, y, ol, sg, *post_w, None, (lng, lnb), ROW_TILE)
        conv_p.append(u[seq - (CONV_WIDTH - 1):].reshape(1, CONV_WIDTH - 1, D_CONV))
        ffn_p.append(up_tail[SUBLANES - (FFN_CONV_WIDTH - 1):].reshape(1, FFN_CONV_WIDTH - 1, 2 * D_FF))
        for g, (window, _) in enumerate(ATTN_GROUPS):
            keep = min(window, seq)
            for j in range(2):
                c0 = (1 + j) * D_ATTN + g * D_GROUP
                kv_p[g][j].append(qkv[seq - keep:, c0:c0 + D_GROUP].reshape(1, keep, HEADS_PER_GROUP, HEAD_DIM))

        us, qkvs, sgs = _inproj(l, xs, g1, w_in_b, n_s * t_new)
        cs, state_s, ols, new_caches = _mixer_sample(l, us, qkvs, state_conv, caches, new_caches, cw, cb, lng, lnb)
        hist = tuple(jnp.repeat(state_ffn_conv[l][:, j], t_new, axis=0) for j in range(FFN_CONV_WIDTH - 1))
        xs, up_s = _post(l, xs, cs, ols, sgs, *post_w, hist, None, n_s * t_new)
        conv_s.append(state_s)
        ffn_s.append(up_s.reshape(n_s, t_new, 2 * D_FF)[:, t_new - (FFN_CONV_WIDTH - 1):])

    st = lambda lst: jnp.stack(lst, axis=0)
    kv_s = [jnp.transpose(nc.reshape(nc.shape[:2] + (HEADS_PER_GROUP, HEAD_DIM, nc.shape[3])), (0, 1, 4, 2, 3))
            for nc in new_caches]
    return (xp.reshape(n_b, seq, D_MODEL), xs.reshape(n_s, t_new, D_MODEL),
            st(conv_p), st(conv_s),
            st(kv_p[0][0]), st(kv_p[0][1]), kv_s[0], kv_s[1],
            st(kv_p[1][0]), st(kv_p[1][1]), kv_s[2], kv_s[3],
            st(kv_p[2][0]), st(kv_p[2][1]), kv_s[4], kv_s[5],
            st(ffn_p), st(ffn_s))
```
